```python
import math
import jax, jax.numpy as jnp
from jax import lax
import numpy as np

D_MODEL = 1024
BATCH = 1
SEQ = 16384
DEPTH = 1
DEC_BATCH = 32
DEC_SEQ = 1
PAST_LEN = 16384
PAGE_SIZE = 128

ATT_WIDTH = D_MODEL // 2
CONV_CH = D_MODEL - ATT_WIDTH
N_HEADS = 8
HEAD_DIM = ATT_WIDTH // N_HEADS
ROT_DIM = HEAD_DIM // 4
ROPE_THETA = 500000.0
MOBA_BLOCK = 256
MOBA_TOPK = 3
Q_CHUNK = 64
CONV_K = 31
N_IN = 3 * ATT_WIDTH + 2 * CONV_CH
PEER_HEADS = 8
PEER_NKEYS = 128
PEER_N = PEER_NKEYS * PEER_NKEYS
PEER_DKEY = 256
PEER_TOPK = 16
PEER_CHUNK = 256
PLE_DIM = 256
NORM_EPS = 1e-6

kernel_name = 'hymba_conformer_moba_peer_step'

F32 = jnp.float32


def rmsnorm(x, g):
    xf = x.astype(F32)
    y = xf * lax.rsqrt(jnp.mean(xf * xf, axis=-1, keepdims=True) + NORM_EPS) * g.astype(F32)
    return y.astype(x.dtype)


def rope_partial(x, pos):
    half = ROT_DIM // 2
    inv = ROPE_THETA ** (-jnp.arange(half, dtype=F32) * 2.0 / ROT_DIM)
    ang = pos[:, None] * inv[None, :]
    cos = jnp.cos(ang)[None, :, None, :]
    sin = jnp.sin(ang)[None, :, None, :]
    xf = x.astype(F32)
    x1 = xf[..., :half]
    x2 = xf[..., half:ROT_DIM]
    out = jnp.concatenate([x1 * cos - x2 * sin, x1 * sin + x2 * cos, xf[..., ROT_DIM:]], axis=-1)
    return out.astype(x.dtype)


def moba_attention(q, k, v, q_pos0):
    B, Sq, H, D = q.shape
    L = k.shape[1]
    nb = -(-L // MOBA_BLOCK)
    pad = nb * MOBA_BLOCK - L
    kb = jnp.pad(k, ((0, 0), (0, pad), (0, 0), (0, 0))).reshape(B, nb, MOBA_BLOCK, H, D)
    vb = jnp.pad(v, ((0, 0), (0, pad), (0, 0), (0, 0))).reshape(B, nb, MOBA_BLOCK, H, D)
    k_mean = jnp.mean(kb.astype(F32), axis=2)
    t_pos = q_pos0 + jnp.arange(Sq, dtype=jnp.int32)
    own = t_pos // MOBA_BLOCK
    gate = jnp.einsum('bshd,bnhd->bshn', q.astype(F32), k_mean)
    fully_past = jnp.arange(nb, dtype=jnp.int32)[None, :] < own[:, None]
    gate = jnp.where(fully_past[None, :, None, :], gate, -jnp.inf)
    n_sel = min(MOBA_TOPK, nb)
    _, sel = lax.top_k(gate, n_sel)
    sel = sel.astype(jnp.int32)
    sel_valid = sel < own[None, :, None, None]
    own_b = jnp.broadcast_to(own[None, :, None, None], (B, Sq, H, 1))
    blk_idx = jnp.concatenate([sel, own_b], axis=-1)
    blk_valid = jnp.concatenate([sel_valid, jnp.ones((B, Sq, H, 1), dtype=bool)], axis=-1)
    kbh = kb.transpose(0, 3, 1, 2, 4)
    vbh = vb.transpose(0, 3, 1, 2, 4)
    qc = math.gcd(Sq, Q_CHUNK)
    n_chunks = Sq // qc
    bi = jnp.arange(B)[:, None, None, None]
    hi = jnp.arange(H)[None, None, :, None]
    scale = D ** -0.5
    offs = jnp.arange(MOBA_BLOCK, dtype=jnp.int32)

    def chunk(c):
        s0 = c * qc
        q_c = lax.dynamic_slice_in_dim(q, s0, qc, axis=1)
        idx_c = lax.dynamic_slice_in_dim(blk_idx, s0, qc, axis=1)
        val_c = lax.dynamic_slice_in_dim(blk_valid, s0, qc, axis=1)
        t_c = q_pos0 + s0 + jnp.arange(qc, dtype=jnp.int32)
        k_g = kbh[bi, hi, idx_c]
        v_g = vbh[bi, hi, idx_c]
        kpos = idx_c[..., None] * MOBA_BLOCK + offs
        mask = val_c[..., None] & (kpos <= t_c[None, :, None, None, None])
        s = jnp.einsum('bqhd,bqhjkd->bqhjk', q_c.astype(F32), k_g.astype(F32)) * scale
        s = jnp.where(mask, s, -jnp.inf)
        p = jax.nn.softmax(s.reshape(B, qc, H, -1), axis=-1).reshape(s.shape)
        o = jnp.einsum('bqhjk,bqhjkd->bqhd', p, v_g.astype(F32))
        return o.astype(q.dtype)

    out = lax.map(chunk, jnp.arange(n_chunks))
    return out.transpose(1, 0, 2, 3, 4).reshape(B, Sq, H, D)


def conformer_conv(glu, buf, w_dw, b_dw, ln_g, ln_b):
    xin = jnp.concatenate([buf.astype(glu.dtype), glu], axis=1)
    y = lax.conv_general_dilated(xin, w_dw[:, None, :].astype(glu.dtype), window_strides=(1,),
                                 padding='VALID', dimension_numbers=('NWC', 'WIO', 'NWC'),
                                 feature_group_count=glu.shape[-1])
    y = (y + b_dw).astype(F32)
    mu = jnp.mean(y, axis=-1, keepdims=True)
    var = jnp.mean(jnp.square(y - mu), axis=-1, keepdims=True)
    y = (y - mu) * lax.rsqrt(var + NORM_EPS) * ln_g.astype(F32) + ln_b.astype(F32)
    y = y * jax.nn.sigmoid(y)
    return y.astype(glu.dtype), xin[:, -(CONV_K - 1):]


def peer(xn, w_q, sub_keys, u_tab, v_tab):
    B, S, Dm = xn.shape
    T = B * S
    t = xn.reshape(T, Dm)
    q = (t @ w_q).reshape(T, PEER_HEADS, 2, PEER_DKEY // 2)
    sc = jnp.einsum('thcd,hcnd->thcn', q.astype(F32), sub_keys.astype(F32))
    s_top, i_top = lax.top_k(sc, PEER_TOPK)
    cand = s_top[:, :, 0, :, None] + s_top[:, :, 1, None, :]
    cand_idx = i_top[:, :, 0, :, None] * PEER_NKEYS + i_top[:, :, 1, None, :]
    best, pos = lax.top_k(cand.reshape(T, PEER_HEADS, PEER_TOPK * PEER_TOPK), PEER_TOPK)
    eidx = jnp.take_along_axis(cand_idx.reshape(T, PEER_HEADS, -1), pos, axis=-1)
    g = jax.nn.softmax(best, axis=-1)
    ch = math.gcd(T, PEER_CHUNK)

    def apply(c):
        t_c = lax.dynamic_slice_in_dim(t, c * ch, ch, 0)
        e_c = lax.dynamic_slice_in_dim(eidx, c * ch, ch, 0)
        g_c = lax.dynamic_slice_in_dim(g, c * ch, ch, 0)
        u = u_tab[e_c]
        v = v_tab[e_c]
        a = jax.nn.gelu(jnp.einsum('td,thkd->thk', t_c.astype(F32), u.astype(F32)), approximate=False)
        return jnp.einsum('thk,thkd->td', g_c * a, v.astype(F32)).astype(xn.dtype)

    out = lax.map(apply, jnp.arange(T // ch))
    return out.reshape(B, S, Dm)


def decoder_layer(h, p_emb, q_pos0, conv_buf, k_past, v_past,
                  g_mix, w_in, w_dw, b_dw, ln_g, ln_b, w_out, g_ffn,
                  peer_wq, peer_keys, peer_u, peer_v, g_ple, w_ple_gate, w_ple_proj):
    B, S, _ = h.shape
    xn = rmsnorm(h, g_mix)
    z = xn @ w_in
    q, k, v, ga, gb = jnp.split(z, [ATT_WIDTH, 2 * ATT_WIDTH, 3 * ATT_WIDTH, 3 * ATT_WIDTH + CONV_CH], axis=-1)
    pos = (q_pos0 + jnp.arange(S, dtype=jnp.int32)).astype(F32)
    q = rope_partial(q.reshape(B, S, N_HEADS, HEAD_DIM), pos)
    k = rope_partial(k.reshape(B, S, N_HEADS, HEAD_DIM), pos)
    v = v.reshape(B, S, N_HEADS, HEAD_DIM)
    if k_past is None:
        k_all, v_all = k, v
    else:
        k_all = jnp.concatenate([k_past.astype(k.dtype), k], axis=1)
        v_all = jnp.concatenate([v_past.astype(v.dtype), v], axis=1)
    attn = moba_attention(q, k_all, v_all, q_pos0).reshape(B, S, ATT_WIDTH)
    glu = ga * jax.nn.sigmoid(gb)
    if conv_buf is None:
        conv_buf = jnp.zeros((B, CONV_K - 1, CONV_CH), dtype=glu.dtype)
    conv_out, new_buf = conformer_conv(glu, conv_buf, w_dw, b_dw, ln_g, ln_b)
    h = h + jnp.concatenate([attn, conv_out], axis=-1) @ w_out
    h = h + peer(rmsnorm(h, g_ffn), peer_wq, peer_keys, peer_u, peer_v)
    gate = jax.nn.sigmoid(rmsnorm(h, g_ple) @ w_ple_gate)
    h = h + gate * (p_emb @ w_ple_proj)
    return h, k, v, new_buf


def setup_inputs(seed: int = 0) -> dict:
    key = jax.random.key(seed)
    ks = jax.random.split(key, 32)
    n_pages = PAST_LEN // PAGE_SIZE
    n_used = DEC_BATCH * n_pages
    n_pool = n_used + max(1, n_used // 4)
    perm = jax.random.permutation(ks[0], n_pool)[:n_used]
    page_table = perm.reshape(DEC_BATCH, n_pages).astype(jnp.int32)
    nrm = lambda k, shape, s: jax.random.normal(k, shape, F32) * s
    return {
        'x_prompt': nrm(ks[1], (BATCH, SEQ, D_MODEL), 1.0),
        'x_sample': nrm(ks[2], (DEC_BATCH, DEC_SEQ, D_MODEL), 1.0),
        'cache_k': nrm(ks[3], (DEPTH, n_pool, PAGE_SIZE, N_HEADS, HEAD_DIM), 1.0),
        'cache_v': nrm(ks[4], (DEPTH, n_pool, PAGE_SIZE, N_HEADS, HEAD_DIM), 1.0),
        'cache_conv': nrm(ks[5], (DEPTH, DEC_BATCH, CONV_K - 1, CONV_CH), 0.5),
        'page_table': page_table,
        'p_prompt': nrm(ks[6], (DEPTH, BATCH, SEQ, PLE_DIM), 1.0),
        'p_sample': nrm(ks[7], (DEPTH, DEC_BATCH, DEC_SEQ, PLE_DIM), 1.0),
        'g_mix': 1.0 + nrm(ks[8], (DEPTH, D_MODEL), 0.01),
        'w_in': nrm(ks[9], (DEPTH, D_MODEL, N_IN), D_MODEL ** -0.5),
        'w_dw': nrm(ks[10], (DEPTH, CONV_K, CONV_CH), CONV_K ** -0.5),
        'b_dw': nrm(ks[11], (DEPTH, CONV_CH), 0.01),
        'ln_g': 1.0 + nrm(ks[12], (DEPTH, CONV_CH), 0.01),
        'ln_b': nrm(ks[13], (DEPTH, CONV_CH), 0.01),
        'w_out': nrm(ks[14], (DEPTH, D_MODEL, D_MODEL), D_MODEL ** -0.5),
        'g_ffn': 1.0 + nrm(ks[15], (DEPTH, D_MODEL), 0.01),
        'peer_wq': nrm(ks[16], (DEPTH, D_MODEL, PEER_HEADS * PEER_DKEY), D_MODEL ** -0.5),
        'peer_keys': nrm(ks[17], (DEPTH, PEER_HEADS, 2, PEER_NKEYS, PEER_DKEY // 2), (PEER_DKEY // 2) ** -0.5),
        'peer_u': nrm(ks[18], (DEPTH, PEER_N, D_MODEL), D_MODEL ** -0.5),
        'peer_v': nrm(ks[19], (DEPTH, PEER_N, D_MODEL), 0.5),
        'g_ple': 1.0 + nrm(ks[20], (DEPTH, D_MODEL), 0.01),
        'w_ple_gate': nrm(ks[21], (DEPTH, D_MODEL, D_MODEL), D_MODEL ** -0.5),
        'w_ple_proj': nrm(ks[22], (DEPTH, PLE_DIM, D_MODEL), PLE_DIM ** -0.5),
        'g_final': 1.0 + nrm(ks[23], (D_MODEL,), 0.01),
    }


def reference(x_prompt, x_sample, cache_k, cache_v, cache_conv, page_table, p_prompt, p_sample,
              g_mix, w_in, w_dw, b_dw, ln_g, ln_b, w_out, g_ffn, peer_wq, peer_keys, peer_u, peer_v,
              g_ple, w_ple_gate, w_ple_proj, g_final):
    n_seq = page_table.shape[0]
    h_p, h_s = x_prompt, x_sample
    kp_l, vp_l, cp_l, ks_l, vs_l, cs_l = [], [], [], [], [], []
    for l in range(DEPTH):
        lw = (g_mix[l], w_in[l], w_dw[l], b_dw[l], ln_g[l], ln_b[l], w_out[l], g_ffn[l],
              peer_wq[l], peer_keys[l], peer_u[l], peer_v[l], g_ple[l], w_ple_gate[l], w_ple_proj[l])
        h_p, k_new, v_new, buf_new = decoder_layer(h_p, p_prompt[l], 0, None, None, None, *lw)
        kp_l.append(k_new); vp_l.append(v_new); cp_l.append(buf_new)
        k_past = cache_k[l][page_table].reshape(n_seq, -1, N_HEADS, HEAD_DIM)
        v_past = cache_v[l][page_table].reshape(n_seq, -1, N_HEADS, HEAD_DIM)
        h_s, k_s, v_s, buf_s = decoder_layer(h_s, p_sample[l], PAST_LEN, cache_conv[l], k_past, v_past, *lw)
        ks_l.append(k_s); vs_l.append(v_s); cs_l.append(buf_s)
    y_prompt = rmsnorm(h_p, g_final)
    y_sample = rmsnorm(h_s, g_final)
    return (y_prompt, y_sample, jnp.stack(kp_l), jnp.stack(vp_l), jnp.stack(cp_l),
            jnp.stack(ks_l), jnp.stack(vs_l), jnp.stack(cs_l))
```

```python
import functools

import jax
import jax.numpy as jnp
from jax import lax
from jax.experimental import pallas as pl
from jax.experimental.pallas import tpu as pltpu

F32 = jnp.float32
BF16 = jnp.bfloat16

D_MODEL = 1024
N_HEADS = 8
HEAD_DIM = 64
ATT_WIDTH = N_HEADS * HEAD_DIM
CONV_CH = D_MODEL - ATT_WIDTH
ROT_DIM = HEAD_DIM // 4
ROPE_THETA = 500000.0
MOBA_BLOCK = 256
MOBA_TOPK = 3
CONV_K = 31
PAGE_SIZE = 128
PEER_HEADS = 8
PEER_NKEYS = 128
PEER_TOPK = 16
PLE_DIM = 256
NORM_EPS = 1e-6

LANES = 128
CONV_HALO = 32
MASKED = -1e30
EXPERT_BLOCK = 2 * PEER_NKEYS
VMEM_LIMIT = 56 * 1024 * 1024


def _cparams(n_axes):
    return pltpu.CompilerParams(dimension_semantics=("arbitrary",) * n_axes, vmem_limit_bytes=VMEM_LIMIT)


def _rmsnorm(x, g):
    return x * lax.rsqrt(jnp.mean(x * x, axis=-1, keepdims=True) + NORM_EPS) * g


def _sigmoid(x):
    return 1.0 / (1.0 + jnp.exp(-x))


def _dot_nt(a, b):
    return lax.dot_general(a, b, (((1,), (1,)), ((), ())), preferred_element_type=F32)


def _rope(x, cos4, sin4, low4):
    half = ROT_DIM // 2
    from_hi = pltpu.roll(x, ATT_WIDTH - half, 1)
    from_lo = pltpu.roll(x, half, 1)
    return x * cos4 + jnp.where(low4, from_hi, from_lo) * sin4


def _inproj_kernel(pos0, pos_stride, with_aux, x_ref, g_ref, w_ref, inv_ref, *out_refs):
    if with_aux:
        k_ref, v_ref, glu_ref, q_ref, kp_ref, vt_ref, kmean_ref = out_refs
    else:
        k_ref, v_ref, glu_ref, q_ref = out_refs
    tm = x_ref.shape[0]
    xn = _rmsnorm(x_ref[...], g_ref[...]).astype(BF16)
    z = jnp.dot(xn, w_ref[...], preferred_element_type=F32)
    q = z[:, 0 * ATT_WIDTH:1 * ATT_WIDTH]
    k = z[:, 1 * ATT_WIDTH:2 * ATT_WIDTH]
    v = z[:, 2 * ATT_WIDTH:3 * ATT_WIDTH]
    ga = z[:, 3 * ATT_WIDTH:3 * ATT_WIDTH + CONV_CH]
    gb = z[:, 3 * ATT_WIDTH + CONV_CH:]

    row = lax.broadcasted_iota(jnp.int32, (tm, LANES), 0)
    d = lax.broadcasted_iota(jnp.int32, (tm, LANES), 1) % HEAD_DIM
    pos = (pos0 + (pl.program_id(0) * tm + row) * pos_stride).astype(F32)
    ang = pos * inv_ref[...]
    cos = jnp.where(d < ROT_DIM, jnp.cos(ang), 1.0)
    sin = jnp.sin(ang)
    sin = jnp.where(d < ROT_DIM // 2, -sin, jnp.where(d < ROT_DIM, sin, 0.0))
    reps = ATT_WIDTH // LANES
    cos4 = jnp.concatenate([cos] * reps, axis=1)
    sin4 = jnp.concatenate([sin] * reps, axis=1)
    low4 = lax.broadcasted_iota(jnp.int32, (tm, ATT_WIDTH), 1) % HEAD_DIM < ROT_DIM // 2

    k = _rope(k, cos4, sin4, low4)
    q = _rope(q, cos4, sin4, low4) * (HEAD_DIM ** -0.5)
    k_ref[...] = k
    v_ref[...] = v
    glu_ref[...] = ga * _sigmoid(gb)
    if with_aux:
        q_ref[...] = q.astype(BF16)
        kp_ref[...] = k.astype(BF16)
        kmean_ref[0] = jnp.mean(k, axis=0, keepdims=True)
        vt = v.T
        for h in range(N_HEADS):
            vt_ref[h, 0] = vt[h * HEAD_DIM:(h + 1) * HEAD_DIM, :].astype(BF16)
    else:
        q_ref[...] = q


def _inproj(x, g_mix, w_in_bf, inv_lane, *, pos0, pos_stride, tm, with_aux):
    t = x.shape[0]
    n_in = w_in_bf.shape[1]
    grid = (t // tm,)
    row_spec = lambda w: pl.BlockSpec((tm, w), lambda i: (i, 0))
    const = lambda shape: pl.BlockSpec(shape, lambda i: (0,) * len(shape))
    out_shape = [jax.ShapeDtypeStruct((t, ATT_WIDTH), F32), jax.ShapeDtypeStruct((t, ATT_WIDTH), F32),
                 jax.ShapeDtypeStruct((t, CONV_CH), F32)]
    out_specs = [row_spec(ATT_WIDTH), row_spec(ATT_WIDTH), row_spec(CONV_CH)]
    if with_aux:
        assert tm == MOBA_BLOCK
        nb = t // MOBA_BLOCK
        out_shape += [jax.ShapeDtypeStruct((t, ATT_WIDTH), BF16), jax.ShapeDtypeStruct((t, ATT_WIDTH), BF16),
                      jax.ShapeDtypeStruct((N_HEADS, nb, HEAD_DIM, MOBA_BLOCK), BF16),
                      jax.ShapeDtypeStruct((nb, 1, ATT_WIDTH), F32)]
        out_specs += [row_spec(ATT_WIDTH), row_spec(ATT_WIDTH),
                      pl.BlockSpec((N_HEADS, 1, HEAD_DIM, MOBA_BLOCK), lambda i: (0, i, 0, 0)),
                      pl.BlockSpec((1, 1, ATT_WIDTH), lambda i: (i, 0, 0))]
    else:
        out_shape += [jax.ShapeDtypeStruct((t, ATT_WIDTH), F32)]
        out_specs += [row_spec(ATT_WIDTH)]
    return pl.pallas_call(
        functools.partial(_inproj_kernel, pos0, pos_stride, with_aux),
        grid=grid,
        in_specs=[row_spec(D_MODEL), const((1, D_MODEL)), const((D_MODEL, n_in)), const((1, LANES))],
        out_specs=out_specs, out_shape=out_shape, compiler_params=_cparams(1),
        name="inproj_aux" if with_aux else "inproj",
    )(x, g_mix, w_in_bf, inv_lane)


def _first_max_onehot(g, idx_f, axis):
    mx = jnp.max(g, axis=axis, keepdims=True)
    first = jnp.min(jnp.where(g == mx, idx_f, jnp.inf), axis=axis, keepdims=True)
    return mx, idx_f == first


def _moba_kernel(q_ref, kp_ref, vt_ref, kmean_ref, o_ref):
    i = pl.program_id(1)
    blk = MOBA_BLOCK
    q2 = q_ref[...]
    km = kmean_ref[...].astype(BF16)
    lane = lax.broadcasted_iota(jnp.int32, (blk, LANES), 1)
    lane_f = lane.astype(F32)
    key_row = lax.broadcasted_iota(jnp.int32, (blk, blk), 0)
    qry_col = lax.broadcasted_iota(jnp.int32, (blk, blk), 1)

    def k_aug(j):
        kj = kp_ref[pl.ds(pl.multiple_of(j * blk, blk), blk), :]
        return jnp.concatenate([kj, (lane == j).astype(BF16)], axis=1)

    q_augs = []
    for hh in range(2):
        qz = jnp.where(lane // HEAD_DIM == hh, q2, jnp.zeros_like(q2))
        gate = _dot_nt(qz, km)
        g = jnp.where(lane < i, gate, -jnp.inf)
        sel = lane == i
        for _ in range(MOBA_TOPK):
            mx, hit = _first_max_onehot(g, lane_f, 1)
            sel = sel | (hit & (mx > -jnp.inf))
            g = jnp.where(hit, -jnp.inf, g)
        bias = jnp.where(sel, 0.0, MASKED).astype(BF16)
        q_augs.append(jnp.concatenate([qz, bias], axis=1))

    ka = k_aug(i)
    state = []
    for hh in range(2):
        s = _dot_nt(ka, q_augs[hh])
        s = jnp.where(key_row <= qry_col, s, MASKED)
        m = jnp.max(s, axis=0, keepdims=True)
        p = jnp.exp(s - m)
        l = jnp.sum(p, axis=0, keepdims=True)
        acc = jnp.dot(vt_ref[hh, i], p.astype(BF16), preferred_element_type=F32)
        state += [m, l, acc]

    def body(j, carry):
        ka = k_aug(j)
        out = []
        for hh in range(2):
            m, l, acc = carry[3 * hh:3 * hh + 3]
            s = _dot_nt(ka, q_augs[hh])
            m_new = jnp.maximum(m, jnp.max(s, axis=0, keepdims=True))
            alpha = jnp.exp(m - m_new)
            p = jnp.exp(s - m_new)
            l = alpha * l + jnp.sum(p, axis=0, keepdims=True)
            acc = alpha * acc + jnp.dot(vt_ref[hh, j], p.astype(BF16), preferred_element_type=F32)
            out += [m_new, l, acc]
        return tuple(out)

    state = lax.fori_loop(0, i, body, tuple(state))
    o_t = jnp.concatenate([state[2] / state[1], state[5] / state[4]], axis=0)
    o_ref[...] = o_t.T.astype(o_ref.dtype)


def _moba_prompt(qp, kp, vt, kmean):
    t = qp.shape[0]
    nb = t // MOBA_BLOCK
    assert t % MOBA_BLOCK == 0 and nb <= LANES
    km = jnp.pad(kmean.reshape(nb, ATT_WIDTH), ((0, LANES - nb), (0, 0)))
    return pl.pallas_call(
        _moba_kernel,
        grid=(N_HEADS // 2, nb),
        in_specs=[pl.BlockSpec((MOBA_BLOCK, LANES), lambda hp, i: (i, hp)),
                  pl.BlockSpec((t, LANES), lambda hp, i: (0, hp)),
                  pl.BlockSpec((2, nb, HEAD_DIM, MOBA_BLOCK), lambda hp, i: (hp, 0, 0, 0)),
                  pl.BlockSpec((LANES, LANES), lambda hp, i: (0, hp))],
        out_specs=pl.BlockSpec((MOBA_BLOCK, LANES), lambda hp, i: (i, hp)),
        out_shape=jax.ShapeDtypeStruct((t, ATT_WIDTH), BF16),
        compiler_params=_cparams(2), name="moba",
    )(qp, kp, vt, km)


def _split3(x):
    a = x.astype(BF16)
    r = x - a.astype(F32)
    b = r.astype(BF16)
    c = (r - b.astype(F32)).astype(BF16)
    return a, b, c


def _moba_dec_kernel(pt_ref, q_ref, kn_ref, vn_ref, k0_ref, k1_ref, v0_ref, v1_ref, o_ref,
                     qrows, gate_s, m_s, l_s, o_s):
    del pt_ref
    n = pl.program_id(1)
    nbp = pl.num_programs(1)
    head_of_lane = lax.broadcasted_iota(jnp.int32, (LANES, ATT_WIDTH), 1) // HEAD_DIM
    row = lax.broadcasted_iota(jnp.int32, (LANES, ATT_WIDTH), 0)
    own_head = head_of_lane == row

    @pl.when(n == 0)
    def _():
        qrows[...] = jnp.where(own_head, q_ref[0], 0.0).astype(BF16)
        gate_s[...] = jnp.full_like(gate_s, -jnp.inf)
        m_s[...] = jnp.zeros_like(m_s)
        l_s[...] = jnp.zeros_like(l_s)
        o_s[...] = jnp.zeros_like(o_s)

    kb = jnp.concatenate([k0_ref[0], k1_ref[0]], axis=0)
    kmean = jnp.mean(kb, axis=0, keepdims=True)
    lhs = jnp.concatenate([kb, jnp.broadcast_to(kmean, (16, ATT_WIDTH))], axis=0).astype(BF16)
    s_all = _dot_nt(lhs, qrows[...])
    s = s_all[:MOBA_BLOCK]
    m = jnp.max(s, axis=0, keepdims=True)
    p = jnp.exp(s - m)
    vb = jnp.concatenate([v0_ref[0], v1_ref[0]], axis=0).astype(BF16)
    o_full = jnp.dot(p.T.astype(BF16), vb, preferred_element_type=F32)
    grp = pl.ds(pl.multiple_of((n // 8) * 8, 8), 8)

    def put_row(ref, val):
        sub = lax.broadcasted_iota(jnp.int32, (8, val.shape[1]), 0)
        ref[grp, :] = jnp.where(sub == n % 8, val, ref[grp, :])

    put_row(gate_s, s_all[MOBA_BLOCK:MOBA_BLOCK + 1])
    put_row(m_s, m)
    put_row(l_s, jnp.sum(p, axis=0, keepdims=True))
    put_row(o_s, jnp.sum(jnp.where(own_head, o_full, 0.0), axis=0, keepdims=True))

    @pl.when(n == nbp - 1)
    def _():
        nb = gate_s.shape[0]
        blk_f = lax.broadcasted_iota(jnp.int32, (nb, LANES), 0).astype(F32)
        g = gate_s[...]
        sel = jnp.zeros((nb, LANES), jnp.bool_)
        for _ in range(MOBA_TOPK):
            mx, hit = _first_max_onehot(g, blk_f, 0)
            sel = sel | (hit & (mx > -jnp.inf))
            g = jnp.where(hit, -jnp.inf, g)
        kn = jnp.broadcast_to(kn_ref[0], (16, ATT_WIDTH)).astype(BF16)
        s_own = _dot_nt(kn, qrows[...])[0:1]
        mm = m_s[...]
        big = jnp.maximum(jnp.max(jnp.where(sel, mm, MASKED), axis=0, keepdims=True), s_own)
        w = jnp.where(sel, jnp.exp(mm - big), 0.0)
        w_own = jnp.exp(s_own - big)
        den = jnp.sum(w * l_s[...], axis=0, keepdims=True) + w_own
        expand = own_head.astype(BF16)
        both = jnp.concatenate([w, jnp.broadcast_to(w_own, (8, LANES)),
                                jnp.broadcast_to(den, (8, LANES))], axis=0)
        wide = sum(jnp.dot(t, expand, preferred_element_type=F32) for t in _split3(both))
        num = jnp.sum(wide[:nb] * o_s[...], axis=0, keepdims=True) + wide[nb:nb + 1] * vn_ref[0]
        o_ref[0] = num / wide[nb + 8:nb + 9]


def _moba_decode(q_s, k_new, v_new, cache_k, cache_v, page_table):
    b = q_s.shape[0]
    n_pages = page_table.shape[1]
    pages_per_blk = MOBA_BLOCK // PAGE_SIZE
    assert pages_per_blk == 2 and n_pages % pages_per_blk == 0
    nbp = n_pages // pages_per_blk
    n_pool = cache_k.shape[0]
    ck = cache_k.reshape(n_pool, PAGE_SIZE, ATT_WIDTH)
    cv = cache_v.reshape(n_pool, PAGE_SIZE, ATT_WIDTH)
    seq3 = lambda a: a.reshape(b, 1, ATT_WIDTH)
    row_spec = pl.BlockSpec((1, 1, ATT_WIDTH), lambda s, n, pt: (s, 0, 0))
    page = lambda which: pl.BlockSpec((1, PAGE_SIZE, ATT_WIDTH),
                                      lambda s, n, pt: (pt[s, pages_per_blk * n + which], 0, 0))
    nb_rows = -(-nbp // 16) * 16
    out = pl.pallas_call(
        _moba_dec_kernel,
        grid_spec=pltpu.PrefetchScalarGridSpec(
            num_scalar_prefetch=1, grid=(b, nbp),
            in_specs=[row_spec, row_spec, row_spec, page(0), page(1), page(0), page(1)],
            out_specs=row_spec,
            scratch_shapes=[pltpu.VMEM((LANES, ATT_WIDTH), BF16), pltpu.VMEM((nb_rows, LANES), F32),
                            pltpu.VMEM((nb_rows, LANES), F32), pltpu.VMEM((nb_rows, LANES), F32),
                            pltpu.VMEM((nb_rows, ATT_WIDTH), F32)]),
        out_shape=jax.ShapeDtypeStruct((b, 1, ATT_WIDTH), F32),
        compiler_params=_cparams(2), name="moba_dec",
    )(page_table, seq3(q_s), seq3(k_new), seq3(v_new), ck, ck, cv, cv)
    return out.reshape(b, ATT_WIDTH)


def _post_conv(y, attn_bf, x, b_dw, ln_g, ln_b, w_out):
    y = y + b_dw
    mu = jnp.mean(y, axis=-1, keepdims=True)
    yc = y - mu
    var = jnp.mean(yc * yc, axis=-1, keepdims=True)
    y = yc * lax.rsqrt(var + NORM_EPS) * ln_g + ln_b
    y = y * _sigmoid(y)
    cat = jnp.concatenate([attn_bf, y.astype(BF16)], axis=1)
    return x + jnp.dot(cat, w_out, preferred_element_type=F32)


def _mix_kernel(glu_ref, halo_ref, attn_ref, x_ref, wdw_ref, bdw_ref, lng_ref, lnb_ref, wout_ref, h_ref, xin):
    tm = glu_ref.shape[0]
    first = pl.program_id(0) == 0
    xin[0:CONV_HALO, :] = jnp.where(first, 0.0, halo_ref[...])
    xin[CONV_HALO:, :] = glu_ref[...]
    off = CONV_HALO - (CONV_K - 1)
    y = jnp.zeros((tm, CONV_CH), F32)
    for k in range(CONV_K):
        y = y + wdw_ref[k:k + 1, :] * xin[off + k:off + k + tm, :]
    h_ref[...] = _post_conv(y, attn_ref[...], x_ref[...], bdw_ref[...], lng_ref[...], lnb_ref[...], wout_ref[...])


def _mix_prompt(glu, attn_bf, x, w_dw, b_dw, ln_g, ln_b, w_out_bf, *, tm):
    t = glu.shape[0]
    assert tm % CONV_HALO == 0 and t % tm == 0
    ratio = tm // CONV_HALO
    const = lambda shape: pl.BlockSpec(shape, lambda i: (0,) * len(shape))
    return pl.pallas_call(
        _mix_kernel,
        grid=(t // tm,),
        in_specs=[pl.BlockSpec((tm, CONV_CH), lambda i: (i, 0)),
                  pl.BlockSpec((CONV_HALO, CONV_CH), lambda i: (jnp.maximum(i * ratio - 1, 0), 0)),
                  pl.BlockSpec((tm, ATT_WIDTH), lambda i: (i, 0)),
                  pl.BlockSpec((tm, D_MODEL), lambda i: (i, 0)),
                  const((CONV_HALO, CONV_CH)), const((1, CONV_CH)), const((1, CONV_CH)), const((1, CONV_CH)),
                  const((D_MODEL, D_MODEL))],
        out_specs=pl.BlockSpec((tm, D_MODEL), lambda i: (i, 0)),
        out_shape=jax.ShapeDtypeStruct((t, D_MODEL), F32),
        scratch_shapes=[pltpu.VMEM((tm + CONV_HALO, CONV_CH), F32)],
        compiler_params=_cparams(1), name="mix",
    )(glu, glu, attn_bf, x, w_dw, b_dw, ln_g, ln_b, w_out_bf)


def _mix_dec_kernel(xin_ref, attn_ref, x_ref, wdw_ref, bdw_ref, lng_ref, lnb_ref, wout_ref, h_ref):
    y = jnp.sum(xin_ref[...] * wdw_ref[...][None], axis=1)
    h_ref[...] = _post_conv(y, attn_ref[...].astype(BF16), x_ref[...], bdw_ref[...], lng_ref[...],
                            lnb_ref[...], wout_ref[...])


def _mix_decode(xin, attn, x, w_dw, b_dw, ln_g, ln_b, w_out_bf):
    b = x.shape[0]
    full = lambda a: pl.BlockSpec(a.shape, lambda i: (0,) * a.ndim)
    args = (xin, attn, x, w_dw, b_dw, ln_g, ln_b, w_out_bf)
    return pl.pallas_call(
        _mix_dec_kernel, grid=(1,), in_specs=[full(a) for a in args],
        out_specs=pl.BlockSpec((b, D_MODEL), lambda i: (0, 0)),
        out_shape=jax.ShapeDtypeStruct((b, D_MODEL), F32),
        compiler_params=_cparams(1), name="mix_dec",
    )(*args)


def _hyperbola_cells():
    return [(i, j) for i in range(PEER_TOPK) for j in range(PEER_TOPK) if (i + 1) * (j + 1) <= PEER_TOPK]


def _route_kernel(h_ref, g_ref, wq_ref, keys_ref, xn_ref, nc1_ref, r2_ref, e1_ref, e2_ref,
                  sc_s, rank_s, top_s, cnt_s, z_s):
    tm = h_ref.shape[0]
    nk = PEER_NKEYS
    xn = _rmsnorm(h_ref[...], g_ref[...]).astype(BF16)
    xn_ref[...] = xn
    qb = jnp.dot(xn, wq_ref[...], preferred_element_type=F32).astype(BF16)
    for hc in range(2 * PEER_HEADS):
        sc_s[hc] = _dot_nt(keys_ref[hc], qb[:, hc * nk:(hc + 1) * nk])

    key_f = lax.broadcasted_iota(jnp.int32, (nk, LANES), 0).astype(F32)
    sub = lax.broadcasted_iota(jnp.int32, (PEER_HEADS, LANES), 0)
    cells = _hyperbola_cells()
    for half in range(tm // LANES):
        cols = slice(half * LANES, (half + 1) * LANES)

        def rank_body(hc, carry):
            x = sc_s[hc, :, cols]
            rank = jnp.full((nk, LANES), float(PEER_TOPK), F32)
            for r in range(PEER_TOPK):
                mx, hit = _first_max_onehot(x, key_f, 0)
                rank = jnp.where(hit, float(r), rank)
                x = jnp.where(hit, -jnp.inf, x)
                top_s[hc, r, :, cols] = jnp.broadcast_to(mx, (8, LANES))
            rank_s[hc, :, cols] = rank
            return carry

        lax.fori_loop(0, 2 * PEER_HEADS, rank_body, 0)

        def heads_on_sublanes(side, r):
            out = top_s[side, r, :, cols]
            for h in range(1, PEER_HEADS):
                out = jnp.where(sub == h, top_s[2 * h + side, r, :, cols], out)
            return out

        a = [heads_on_sublanes(0, r) for r in range(PEER_TOPK)]
        bb = [heads_on_sublanes(1, r) for r in range(PEER_TOPK)]
        cur0 = tuple(a[i] + bb[j] for (i, j) in cells)
        best = cur0[0]

        def pick_body(_, carry):
            cur, taken = carry
            mx = functools.reduce(jnp.maximum, cur)
            found = jnp.zeros_like(mx)
            new_cur, new_taken = [], []
            for ci in range(len(cells)):
                hit = jnp.where(cur[ci] == mx, 1.0 - found, 0.0)
                found = found + hit
                new_taken.append(taken[ci] + hit)
                new_cur.append(jnp.where(hit > 0.0, -jnp.inf, cur[ci]))
            return tuple(new_cur), tuple(new_taken)

        _, taken = lax.fori_loop(0, PEER_TOPK, pick_body,
                                 (cur0, tuple(jnp.zeros_like(best) for _ in cells)))
        z = jnp.zeros_like(best)
        counts = [jnp.zeros_like(best) for _ in range(PEER_TOPK)]
        for ci, (i, j) in enumerate(cells):
            counts[i] = counts[i] + taken[ci]
            z = z + taken[ci] * jnp.exp(a[i] + bb[j] - best)
        for h in range(PEER_HEADS):
            z_s[h, :, cols] = jnp.broadcast_to(z[h:h + 1], (8, LANES))
            for r in range(PEER_TOPK):
                cnt_s[h, r, :, cols] = jnp.broadcast_to(counts[r][h:h + 1], (8, LANES))

        def gate_body(h, carry):
            r1 = rank_s[2 * h, :, cols]
            nc = jnp.zeros((nk, LANES), F32)
            for r in range(PEER_TOPK):
                nc = jnp.where(r1 == float(r), cnt_s[h, r, 0:1, cols], nc)
            nc1_ref[h, :, cols] = nc
            r2_ref[h, :, cols] = rank_s[2 * h + 1, :, cols]
            a0 = top_s[2 * h, 0, 0:1, cols]
            b0 = top_s[2 * h + 1, 0, 0:1, cols]
            e1_ref[h, :, cols] = jnp.exp(sc_s[2 * h, :, cols] - a0) / z_s[h, 0:1, cols]
            e2_ref[h, :, cols] = jnp.exp(sc_s[2 * h + 1, :, cols] - b0)
            return carry

        lax.fori_loop(0, PEER_HEADS, gate_body, 0)


def _peer_route(h1, g_ffn, wq_bf, keys_bf, *, tm):
    t = h1.shape[0]
    assert t % tm == 0 and tm % LANES == 0
    const = lambda shape: pl.BlockSpec(shape, lambda i: (0,) * len(shape))
    gate_shape = jax.ShapeDtypeStruct((PEER_HEADS, PEER_NKEYS, t), F32)
    gate_spec = pl.BlockSpec((PEER_HEADS, PEER_NKEYS, tm), lambda i: (0, 0, i))
    return pl.pallas_call(
        _route_kernel,
        grid=(t // tm,),
        in_specs=[pl.BlockSpec((tm, D_MODEL), lambda i: (i, 0)), const((1, D_MODEL)),
                  const(wq_bf.shape), const(keys_bf.shape)],
        out_specs=[pl.BlockSpec((tm, D_MODEL), lambda i: (i, 0))] + [gate_spec] * 4,
        out_shape=[jax.ShapeDtypeStruct((t, D_MODEL), BF16)] + [gate_shape] * 4,
        scratch_shapes=[pltpu.VMEM((2 * PEER_HEADS, PEER_NKEYS, tm), F32),
                        pltpu.VMEM((2 * PEER_HEADS, PEER_NKEYS, tm), F32),
                        pltpu.VMEM((2 * PEER_HEADS, PEER_TOPK, 8, tm), F32),
                        pltpu.VMEM((PEER_HEADS, PEER_TOPK, 8, tm), F32),
                        pltpu.VMEM((PEER_HEADS, 8, tm), F32)],
        compiler_params=_cparams(1), name="route",
    )(h1, g_ffn, wq_bf, keys_bf)


def _experts_kernel(xn_ref, u_ref, vt_ref, nc1_ref, r2_ref, e1_ref, e2_ref, h_ref, o_ref, acc):
    c = pl.program_id(1)

    @pl.when(c == 0)
    def _():
        acc[...] = jnp.zeros_like(acc)

    pre = _dot_nt(u_ref[...], xn_ref[...])
    act = 0.5 * pre * (1.0 + lax.erf(pre * (2.0 ** -0.5)))
    halves = []
    for u in range(EXPERT_BLOCK // PEER_NKEYS):
        i1 = c * (EXPERT_BLOCK // PEER_NKEYS) + u
        g = jnp.zeros((PEER_NKEYS, act.shape[1]), F32)
        for h in range(PEER_HEADS):
            nc = nc1_ref[h, pl.ds(i1, 1), :]
            e1 = e1_ref[h, pl.ds(i1, 1), :]
            g = g + jnp.where(r2_ref[h] < nc, e2_ref[h], 0.0) * e1
        halves.append(g * act[u * PEER_NKEYS:(u + 1) * PEER_NKEYS])
    w = jnp.concatenate(halves, axis=0).astype(BF16)
    acc[...] += jnp.dot(vt_ref[...], w, preferred_element_type=F32)

    @pl.when(c == pl.num_programs(1) - 1)
    def _():
        o_ref[...] = h_ref[...] + acc[...].T


def _peer_experts(xn, u_bf, vt_bf, nc1, r2, e1, e2, h1, *, tm):
    t = xn.shape[0]
    n_exp = u_bf.shape[0]
    assert t % tm == 0 and n_exp % EXPERT_BLOCK == 0
    gate_spec = pl.BlockSpec((PEER_HEADS, PEER_NKEYS, tm), lambda i, c: (0, 0, i))
    tok_spec = pl.BlockSpec((tm, D_MODEL), lambda i, c: (i, 0))
    return pl.pallas_call(
        _experts_kernel,
        grid=(t // tm, n_exp // EXPERT_BLOCK),
        in_specs=[tok_spec,
                  pl.BlockSpec((EXPERT_BLOCK, D_MODEL), lambda i, c: (c, 0)),
                  pl.BlockSpec((D_MODEL, EXPERT_BLOCK), lambda i, c: (0, c)),
                  gate_spec, gate_spec, gate_spec, gate_spec, tok_spec],
        out_specs=tok_spec,
        out_shape=jax.ShapeDtypeStruct((t, D_MODEL), F32),
        scratch_shapes=[pltpu.VMEM((D_MODEL, tm), F32)],
        compiler_params=_cparams(2), name="experts",
    )(xn, u_bf, vt_bf, nc1, r2, e1, e2, h1)


def _ple_kernel(h_ref, p_ref, gple_ref, wg_ref, wp_ref, gfin_ref, y_ref):
    h = h_ref[...]
    gate = _sigmoid(jnp.dot(_rmsnorm(h, gple_ref[...]).astype(BF16), wg_ref[...], preferred_element_type=F32))
    proj = jnp.dot(p_ref[...].astype(BF16), wp_ref[...], preferred_element_type=F32)
    y_ref[...] = _rmsnorm(h + gate * proj, gfin_ref[...])


def _ple_final(h2, p_emb, g_ple, wg_bf, wp_bf, g_final, *, tm):
    t = h2.shape[0]
    const = lambda shape: pl.BlockSpec(shape, lambda i: (0,) * len(shape))
    return pl.pallas_call(
        _ple_kernel,
        grid=(t // tm,),
        in_specs=[pl.BlockSpec((tm, D_MODEL), lambda i: (i, 0)), pl.BlockSpec((tm, PLE_DIM), lambda i: (i, 0)),
                  const((1, D_MODEL)), const((D_MODEL, D_MODEL)), const((PLE_DIM, D_MODEL)), const((1, D_MODEL))],
        out_specs=pl.BlockSpec((tm, D_MODEL), lambda i: (i, 0)),
        out_shape=jax.ShapeDtypeStruct((t, D_MODEL), F32),
        compiler_params=_cparams(1), name="ple",
    )(h2, p_emb, g_ple, wg_bf, wp_bf, g_final)


def _pick_tile(t, want):
    tm = min(want, t)
    assert t % tm == 0
    return tm


def _ffn(h1, p_emb, lw, g_final):
    t = h1.shape[0]
    xn, nc1, r2, e1, e2 = _peer_route(h1, lw["g_ffn"], lw["wq"], lw["keys"], tm=_pick_tile(t, 256))
    h2 = _peer_experts(xn, lw["u"], lw["vt"], nc1, r2, e1, e2, h1, tm=_pick_tile(t, 512))
    return _ple_final(h2, p_emb, lw["g_ple"], lw["w_ple_gate"], lw["w_ple_proj"], g_final, tm=_pick_tile(t, 256))


def kernel(x_prompt, x_sample, cache_k, cache_v, cache_conv, page_table, p_prompt, p_sample,
           g_mix, w_in, w_dw, b_dw, ln_g, ln_b, w_out, g_ffn, peer_wq, peer_keys, peer_u, peer_v,
           g_ple, w_ple_gate, w_ple_proj, g_final):
    depth = g_mix.shape[0]
    batch, seq, _ = x_prompt.shape
    dec_batch, dec_seq, _ = x_sample.shape
    assert depth == 1 and batch == 1 and dec_seq == 1
    past_len = page_table.shape[1] * PAGE_SIZE
    assert seq % MOBA_BLOCK == 0 and past_len % MOBA_BLOCK == 0 and seq >= CONV_K - 1
    row = lambda a: a.reshape(1, -1)
    lane = jnp.arange(LANES) % (ROT_DIM // 2)
    inv_lane = (ROPE_THETA ** (-lane.astype(F32) * 2.0 / ROT_DIM)).reshape(1, LANES)
    l = 0
    lw = dict(
        g_ffn=row(g_ffn[l]), wq=peer_wq[l].astype(BF16),
        keys=peer_keys[l].reshape(2 * PEER_HEADS, PEER_NKEYS, -1).astype(BF16),
        u=peer_u[l].astype(BF16), vt=peer_v[l].T.astype(BF16),
        g_ple=row(g_ple[l]), w_ple_gate=w_ple_gate[l].astype(BF16), w_ple_proj=w_ple_proj[l].astype(BF16))
    w_in_bf = w_in[l].astype(BF16)
    w_out_bf = w_out[l].astype(BF16)
    w_dw_pad = jnp.pad(w_dw[l], ((0, CONV_HALO - CONV_K), (0, 0)))
    conv_args = (w_dw_pad, row(b_dw[l]), row(ln_g[l]), row(ln_b[l]), w_out_bf)
    g_fin = row(g_final)

    xp = x_prompt.reshape(seq, D_MODEL)
    k_p, v_p, glu_p, qp, kp, vt, kmean = _inproj(xp, row(g_mix[l]), w_in_bf, inv_lane, pos0=0, pos_stride=1,
                                                 tm=MOBA_BLOCK, with_aux=True)
    attn_p = _moba_prompt(qp, kp, vt, kmean)
    h1_p = _mix_prompt(glu_p, attn_p, xp, *conv_args, tm=_pick_tile(seq, 256))
    y_p = _ffn(h1_p, p_prompt[l].reshape(seq, PLE_DIM), lw, g_fin)

    xs = x_sample.reshape(dec_batch, D_MODEL)
    k_s, v_s, glu_s, q_s = _inproj(xs, row(g_mix[l]), w_in_bf, inv_lane, pos0=past_len, pos_stride=0,
                                   tm=dec_batch, with_aux=False)
    attn_s = _moba_decode(q_s, k_s, v_s, cache_k[l], cache_v[l], page_table)
    xin_s = jnp.concatenate([cache_conv[l], glu_s[:, None, :]], axis=1)
    xin_pad = jnp.pad(xin_s, ((0, 0), (0, CONV_HALO - CONV_K), (0, 0)))
    h1_s = _mix_decode(xin_pad, attn_s, xs, *conv_args)
    pad = (-dec_batch) % LANES
    h1_s = jnp.pad(h1_s, ((0, pad), (0, 0)))
    p_s = jnp.pad(p_sample[l].reshape(dec_batch, PLE_DIM), ((0, pad), (0, 0)))
    y_s = _ffn(h1_s, p_s, lw, g_fin)[:dec_batch]

    conv_p = glu_p[seq - (CONV_K - 1):].reshape(1, 1, CONV_K - 1, CONV_CH)
    conv_s = xin_s[:, 1:].reshape(1, dec_batch, CONV_K - 1, CONV_CH)
    return (y_p.reshape(1, seq, D_MODEL), y_s.reshape(dec_batch, 1, D_MODEL),
            k_p.reshape(1, 1, seq, N_HEADS, HEAD_DIM), v_p.reshape(1, 1, seq, N_HEADS, HEAD_DIM), conv_p,
            k_s.reshape(1, dec_batch, 1, N_HEADS, HEAD_DIM), v_s.reshape(1, dec_batch, 1, N_HEADS, HEAD_DIM),
            conv_s)
```

```python
import functools

import jax
import jax.numpy as jnp
from jax import lax
from jax.experimental import pallas as pl
from jax.experimental.pallas import tpu as pltpu

F32 = jnp.float32
BF16 = jnp.bfloat16

D_MODEL = 1024
N_HEADS = 8
HEAD_DIM = 64
ATT_WIDTH = N_HEADS * HEAD_DIM
CONV_CH = D_MODEL - ATT_WIDTH
ROT_DIM = HEAD_DIM // 4
ROPE_THETA = 500000.0
MOBA_BLOCK = 256
MOBA_TOPK = 3
CONV_K = 31
PAGE_SIZE = 128
PEER_HEADS = 8
PEER_NKEYS = 128
PEER_TOPK = 16
PLE_DIM = 256
NORM_EPS = 1e-6

LANES = 128
CONV_HALO = 32
MASKED = -1e30
EXPERT_BLOCK = 2 * PEER_NKEYS
VMEM_LIMIT = 56 * 1024 * 1024


def _cparams(n_axes):
    return pltpu.CompilerParams(dimension_semantics=("arbitrary",) * n_axes, vmem_limit_bytes=VMEM_LIMIT)


def _rmsnorm(x, g):
    return x * lax.rsqrt(jnp.mean(x * x, axis=-1, keepdims=True) + NORM_EPS) * g


def _sigmoid(x):
    return 1.0 / (1.0 + jnp.exp(-x))


def _dot_nt(a, b):
    return lax.dot_general(a, b, (((1,), (1,)), ((), ())), preferred_element_type=F32)


def _rope(x, cos4, sin4, low4):
    half = ROT_DIM // 2
    from_hi = pltpu.roll(x, ATT_WIDTH - half, 1)
    from_lo = pltpu.roll(x, half, 1)
    return x * cos4 + jnp.where(low4, from_hi, from_lo) * sin4


def _inproj_kernel(pos0, pos_stride, with_aux, x_ref, g_ref, w_ref, inv_ref, *out_refs):
    if with_aux:
        k_ref, v_ref, glu_ref, q_ref, kp_ref, vt_ref, kmean_ref = out_refs
    else:
        k_ref, v_ref, glu_ref, q_ref = out_refs
    tm = x_ref.shape[0]
    xn = _rmsnorm(x_ref[...], g_ref[...]).astype(BF16)
    z = jnp.dot(xn, w_ref[...], preferred_element_type=F32)
    q = z[:, 0 * ATT_WIDTH:1 * ATT_WIDTH]
    k = z[:, 1 * ATT_WIDTH:2 * ATT_WIDTH]
    v = z[:, 2 * ATT_WIDTH:3 * ATT_WIDTH]
    ga = z[:, 3 * ATT_WIDTH:3 * ATT_WIDTH + CONV_CH]
    gb = z[:, 3 * ATT_WIDTH + CONV_CH:]

    row = lax.broadcasted_iota(jnp.int32, (tm, LANES), 0)
    d = lax.broadcasted_iota(jnp.int32, (tm, LANES), 1) % HEAD_DIM
    pos = (pos0 + (pl.program_id(0) * tm + row) * pos_stride).astype(F32)
    ang = pos * inv_ref[...]
    cos = jnp.where(d < ROT_DIM, jnp.cos(ang), 1.0)
    sin = jnp.sin(ang)
    sin = jnp.where(d < ROT_DIM // 2, -sin, jnp.where(d < ROT_DIM, sin, 0.0))
    reps = ATT_WIDTH // LANES
    cos4 = jnp.concatenate([cos] * reps, axis=1)
    sin4 = jnp.concatenate([sin] * reps, axis=1)
    low4 = lax.broadcasted_iota(jnp.int32, (tm, ATT_WIDTH), 1) % HEAD_DIM < ROT_DIM // 2

    k = _rope(k, cos4, sin4, low4)
    q = _rope(q, cos4, sin4, low4) * (HEAD_DIM ** -0.5)
    k_ref[...] = k
    v_ref[...] = v
    glu_ref[...] = ga * _sigmoid(gb)
    if with_aux:
        q_ref[...] = q.astype(BF16)
        kp_ref[...] = k.astype(BF16)
        kmean_ref[0] = jnp.mean(k, axis=0, keepdims=True)
        vt = v.T
        for h in range(N_HEADS):
            vt_ref[h, 0] = vt[h * HEAD_DIM:(h + 1) * HEAD_DIM, :].astype(BF16)
    else:
        q_ref[...] = q


def _inproj(x, g_mix, w_in_bf, inv_lane, *, pos0, pos_stride, tm, with_aux):
    t = x.shape[0]
    n_in = w_in_bf.shape[1]
    grid = (t // tm,)
    row_spec = lambda w: pl.BlockSpec((tm, w), lambda i: (i, 0))
    const = lambda shape: pl.BlockSpec(shape, lambda i: (0,) * len(shape))
    out_shape = [jax.ShapeDtypeStruct((t, ATT_WIDTH), F32), jax.ShapeDtypeStruct((t, ATT_WIDTH), F32),
                 jax.ShapeDtypeStruct((t, CONV_CH), F32)]
    out_specs = [row_spec(ATT_WIDTH), row_spec(ATT_WIDTH), row_spec(CONV_CH)]
    if with_aux:
        assert tm == MOBA_BLOCK
        nb = t // MOBA_BLOCK
        out_shape += [jax.ShapeDtypeStruct((t, ATT_WIDTH), BF16), jax.ShapeDtypeStruct((t, ATT_WIDTH), BF16),
                      jax.ShapeDtypeStruct((N_HEADS, nb, HEAD_DIM, MOBA_BLOCK), BF16),
                      jax.ShapeDtypeStruct((nb, 1, ATT_WIDTH), F32)]
        out_specs += [row_spec(ATT_WIDTH), row_spec(ATT_WIDTH),
                      pl.BlockSpec((N_HEADS, 1, HEAD_DIM, MOBA_BLOCK), lambda i: (0, i, 0, 0)),
                      pl.BlockSpec((1, 1, ATT_WIDTH), lambda i: (i, 0, 0))]
    else:
        out_shape += [jax.ShapeDtypeStruct((t, ATT_WIDTH), F32)]
        out_specs += [row_spec(ATT_WIDTH)]
    return pl.pallas_call(
        functools.partial(_inproj_kernel, pos0, pos_stride, with_aux),
        grid=grid,
        in_specs=[row_spec(D_MODEL), const((1, D_MODEL)), const((D_MODEL, n_in)), const((1, LANES))],
        out_specs=out_specs, out_shape=out_shape, compiler_params=_cparams(1),
        name="inproj_aux" if with_aux else "inproj",
    )(x, g_mix, w_in_bf, inv_lane)


def _first_max_onehot(g, idx_f, axis):
    mx = jnp.max(g, axis=axis, keepdims=True)
    first = jnp.min(jnp.where(g == mx, idx_f, jnp.inf), axis=axis, keepdims=True)
    return mx, idx_f == first


def _moba_kernel(chunk, q_ref, kp_ref, vt_ref, kmean_ref, o_ref):
    i = pl.program_id(1)
    blk = MOBA_BLOCK
    q2 = q_ref[...]
    km = kmean_ref[...].astype(BF16)
    lane = lax.broadcasted_iota(jnp.int32, (blk, LANES), 1)
    lane_f = lane.astype(F32)
    key_row = lax.broadcasted_iota(jnp.int32, (blk, blk), 0)
    qry_col = lax.broadcasted_iota(jnp.int32, (blk, blk), 1)

    qzs, biases = [], []
    for hh in range(2):
        qz = jnp.where(lane // HEAD_DIM == hh, q2, jnp.zeros_like(q2))
        gate = _dot_nt(qz, km)
        g = jnp.where(lane < i, gate, -jnp.inf)
        sel = jnp.zeros((blk, LANES), jnp.bool_)
        for _ in range(MOBA_TOPK):
            mx, hit = _first_max_onehot(g, lane_f, 1)
            sel = sel | (hit & (mx > -jnp.inf))
            g = jnp.where(hit, -jnp.inf, g)
        qzs.append(qz)
        biases.append(jnp.where(sel, 0.0, MASKED))

    k_own = kp_ref[pl.ds(pl.multiple_of(i * blk, blk), blk), :]
    state = []
    for hh in range(2):
        s = _dot_nt(k_own, qzs[hh])
        s = jnp.where(key_row <= qry_col, s, MASKED)
        m = jnp.max(s, axis=0, keepdims=True)
        p = jnp.exp(s - m)
        l = jnp.sum(p, axis=0, keepdims=True)
        acc = jnp.dot(vt_ref[hh, i], p.astype(BF16), preferred_element_type=F32)
        state += [m, l, acc]

    def body(c, carry):
        base = c * chunk
        shift = lax.rem(LANES - base, LANES)
        k_augs = []
        for r in range(chunk):
            kr = kp_ref[pl.ds(pl.multiple_of((base + r) * blk, blk), blk), :]
            k_augs.append(jnp.concatenate([kr, (lane == r).astype(BF16)], axis=1))
        out = []
        for hh in range(2):
            m, l, acc = carry[3 * hh:3 * hh + 3]
            bias_c = pltpu.roll(biases[hh], shift, 1).astype(BF16)
            q_aug = jnp.concatenate([qzs[hh], bias_c], axis=1)
            ss = [_dot_nt(ka, q_aug) for ka in k_augs]
            mc = functools.reduce(jnp.maximum, [jnp.max(s, axis=0, keepdims=True) for s in ss])
            m_new = jnp.maximum(m, mc)
            alpha = jnp.exp(m - m_new)
            l = alpha * l
            acc = alpha * acc
            for r in range(chunk):
                p = jnp.exp(ss[r] - m_new)
                l = l + jnp.sum(p, axis=0, keepdims=True)
                acc = acc + jnp.dot(vt_ref[hh, base + r], p.astype(BF16), preferred_element_type=F32)
            out += [m_new, l, acc]
        return tuple(out)

    state = lax.fori_loop(0, (i + chunk - 1) // chunk, body, tuple(state))
    o_t = jnp.concatenate([state[2] / state[1], state[5] / state[4]], axis=0)
    o_ref[...] = o_t.T.astype(o_ref.dtype)


def _moba_prompt(qp, kp, vt, kmean):
    t = qp.shape[0]
    nb = t // MOBA_BLOCK
    assert t % MOBA_BLOCK == 0 and nb <= LANES
    chunk = next(c for c in (8, 4, 2, 1) if nb % c == 0)
    km = jnp.pad(kmean.reshape(nb, ATT_WIDTH), ((0, LANES - nb), (0, 0)))
    return pl.pallas_call(
        functools.partial(_moba_kernel, chunk),
        grid=(N_HEADS // 2, nb),
        in_specs=[pl.BlockSpec((MOBA_BLOCK, LANES), lambda hp, i: (i, hp)),
                  pl.BlockSpec((t, LANES), lambda hp, i: (0, hp)),
                  pl.BlockSpec((2, nb, HEAD_DIM, MOBA_BLOCK), lambda hp, i: (hp, 0, 0, 0)),
                  pl.BlockSpec((LANES, LANES), lambda hp, i: (0, hp))],
        out_specs=pl.BlockSpec((MOBA_BLOCK, LANES), lambda hp, i: (i, hp)),
        out_shape=jax.ShapeDtypeStruct((t, ATT_WIDTH), BF16),
        compiler_params=_cparams(2), name="moba",
    )(qp, kp, vt, km)


def _moba_dec_kernel(pt_ref, q_ref, kn_ref, vn_ref, k0_ref, k1_ref, v0_ref, v1_ref, o_ref,
                     gate_s, m_s, l_s, o_s):
    del pt_ref
    n = pl.program_id(1)
    nbp = pl.num_programs(1)
    blk = MOBA_BLOCK
    rows = blk * N_HEADS
    blk_lane = lax.broadcasted_iota(jnp.int32, (N_HEADS, LANES), 1)

    @pl.when(n == 0)
    def _():
        gate_s[...] = jnp.full_like(gate_s, -jnp.inf)
        m_s[...] = jnp.zeros_like(m_s)
        l_s[...] = jnp.zeros_like(l_s)

    q8 = q_ref[0]
    q16 = jnp.concatenate([q8, jnp.zeros_like(q8)], axis=0).astype(BF16)
    kb = jnp.concatenate([k0_ref[0], k1_ref[0]], axis=0)
    kmean = jnp.mean(kb, axis=0)
    pad = jnp.zeros((LANES - N_HEADS, HEAD_DIM), F32)
    keys = jnp.concatenate([kb.reshape(rows, HEAD_DIM), kmean, pad], axis=0).astype(BF16)
    s_t = _dot_nt(q16, keys)[:N_HEADS]
    col = lax.broadcasted_iota(jnp.int32, (N_HEADS, rows + LANES), 1)
    own = col % N_HEADS == lax.broadcasted_iota(jnp.int32, (N_HEADS, rows + LANES), 0)
    s_t = jnp.where(own & (col < rows + N_HEADS), s_t, -jnp.inf)
    s = s_t[:, :rows]
    gate = jnp.max(s_t[:, rows:], axis=1, keepdims=True)
    m = jnp.max(s, axis=1, keepdims=True)
    p = jnp.exp(s - m)
    p16 = jnp.concatenate([p, jnp.zeros_like(p)], axis=0).astype(BF16)
    vb = jnp.concatenate([v0_ref[0], v1_ref[0]], axis=0).reshape(rows, HEAD_DIM).astype(BF16)
    o_s[n] = jnp.dot(p16, vb, preferred_element_type=F32)[:N_HEADS]
    here = blk_lane == n
    gate_s[...] = jnp.where(here, gate, gate_s[...])
    m_s[...] = jnp.where(here, m, m_s[...])
    l_s[...] = jnp.where(here, jnp.sum(p, axis=1, keepdims=True), l_s[...])

    @pl.when(n == nbp - 1)
    def _():
        g = gate_s[...]
        sel = jnp.zeros((N_HEADS, LANES), jnp.bool_)
        for _ in range(MOBA_TOPK):
            mx, hit = _first_max_onehot(g, blk_lane.astype(F32), 1)
            sel = sel | (hit & (mx > -jnp.inf))
            g = jnp.where(hit, -jnp.inf, g)
        as_mxu = lambda a: a.astype(BF16).astype(F32)
        s_own = jnp.sum(as_mxu(kn_ref[0]) * as_mxu(q8), axis=1, keepdims=True)
        mm = m_s[...]
        big = jnp.maximum(jnp.max(jnp.where(sel, mm, MASKED), axis=1, keepdims=True), s_own)
        w = jnp.where(sel, jnp.exp(mm - big), 0.0)
        w_own = jnp.exp(s_own - big)
        den = jnp.sum(w * l_s[...], axis=1, keepdims=True) + w_own
        num = w_own * vn_ref[0]
        for nn in range(o_s.shape[0]):
            num = num + w[:, nn:nn + 1] * o_s[nn]
        o_ref[0] = num / den


def _moba_decode(q_s, k_new, v_new, cache_k, cache_v, page_table):
    b = q_s.shape[0]
    n_pages = page_table.shape[1]
    pages_per_blk = MOBA_BLOCK // PAGE_SIZE
    assert pages_per_blk == 2 and n_pages % pages_per_blk == 0
    nbp = n_pages // pages_per_blk
    assert nbp <= LANES
    heads = lambda a: a.reshape(b, N_HEADS, HEAD_DIM)
    row_spec = pl.BlockSpec((1, N_HEADS, HEAD_DIM), lambda s, n, pt: (s, 0, 0))
    page = lambda which: pl.BlockSpec((1, PAGE_SIZE, N_HEADS, HEAD_DIM),
                                      lambda s, n, pt: (pt[s, pages_per_blk * n + which], 0, 0, 0))
    out = pl.pallas_call(
        _moba_dec_kernel,
        grid_spec=pltpu.PrefetchScalarGridSpec(
            num_scalar_prefetch=1, grid=(b, nbp),
            in_specs=[row_spec, row_spec, row_spec, page(0), page(1), page(0), page(1)],
            out_specs=row_spec,
            scratch_shapes=[pltpu.VMEM((N_HEADS, LANES), F32), pltpu.VMEM((N_HEADS, LANES), F32),
                            pltpu.VMEM((N_HEADS, LANES), F32), pltpu.VMEM((nbp, N_HEADS, HEAD_DIM), F32)]),
        out_shape=jax.ShapeDtypeStruct((b, N_HEADS, HEAD_DIM), F32),
        compiler_params=_cparams(2), name="moba_dec",
    )(page_table, heads(q_s), heads(k_new), heads(v_new), cache_k, cache_k, cache_v, cache_v)
    return out.reshape(b, ATT_WIDTH)


def _post_conv(y, attn_bf, x, b_dw, ln_g, ln_b, w_out):
    y = y + b_dw
    mu = jnp.mean(y, axis=-1, keepdims=True)
    yc = y - mu
    var = jnp.mean(yc * yc, axis=-1, keepdims=True)
    y = yc * lax.rsqrt(var + NORM_EPS) * ln_g + ln_b
    y = y * _sigmoid(y)
    cat = jnp.concatenate([attn_bf, y.astype(BF16)], axis=1)
    return x + jnp.dot(cat, w_out, preferred_element_type=F32)


def _mix_kernel(glu_ref, halo_ref, attn_ref, x_ref, wdw_ref, bdw_ref, lng_ref, lnb_ref, wout_ref, h_ref, xin):
    tm = glu_ref.shape[0]
    first = pl.program_id(0) == 0
    xin[0:CONV_HALO, :] = jnp.where(first, 0.0, halo_ref[...])
    xin[CONV_HALO:, :] = glu_ref[...]
    off = CONV_HALO - (CONV_K - 1)
    y = jnp.zeros((tm, CONV_CH), F32)
    for k in range(CONV_K):
        y = y + wdw_ref[k:k + 1, :] * xin[off + k:off + k + tm, :]
    h_ref[...] = _post_conv(y, attn_ref[...], x_ref[...], bdw_ref[...], lng_ref[...], lnb_ref[...], wout_ref[...])


def _mix_prompt(glu, attn_bf, x, w_dw, b_dw, ln_g, ln_b, w_out_bf, *, tm):
    t = glu.shape[0]
    assert tm % CONV_HALO == 0 and t % tm == 0
    ratio = tm // CONV_HALO
    const = lambda shape: pl.BlockSpec(shape, lambda i: (0,) * len(shape))
    return pl.pallas_call(
        _mix_kernel,
        grid=(t // tm,),
        in_specs=[pl.BlockSpec((tm, CONV_CH), lambda i: (i, 0)),
                  pl.BlockSpec((CONV_HALO, CONV_CH), lambda i: (jnp.maximum(i * ratio - 1, 0), 0)),
                  pl.BlockSpec((tm, ATT_WIDTH), lambda i: (i, 0)),
                  pl.BlockSpec((tm, D_MODEL), lambda i: (i, 0)),
                  const((CONV_HALO, CONV_CH)), const((1, CONV_CH)), const((1, CONV_CH)), const((1, CONV_CH)),
                  const((D_MODEL, D_MODEL))],
        out_specs=pl.BlockSpec((tm, D_MODEL), lambda i: (i, 0)),
        out_shape=jax.ShapeDtypeStruct((t, D_MODEL), F32),
        scratch_shapes=[pltpu.VMEM((tm + CONV_HALO, CONV_CH), F32)],
        compiler_params=_cparams(1), name="mix",
    )(glu, glu, attn_bf, x, w_dw, b_dw, ln_g, ln_b, w_out_bf)


def _mix_dec_kernel(xin_ref, attn_ref, x_ref, wdw_ref, bdw_ref, lng_ref, lnb_ref, wout_ref, h_ref):
    y = jnp.sum(xin_ref[...] * wdw_ref[...][None], axis=1)
    h_ref[...] = _post_conv(y, attn_ref[...].astype(BF16), x_ref[...], bdw_ref[...], lng_ref[...],
                            lnb_ref[...], wout_ref[...])


def _mix_decode(xin, attn, x, w_dw, b_dw, ln_g, ln_b, w_out_bf):
    b = x.shape[0]
    full = lambda a: pl.BlockSpec(a.shape, lambda i: (0,) * a.ndim)
    args = (xin, attn, x, w_dw, b_dw, ln_g, ln_b, w_out_bf)
    return pl.pallas_call(
        _mix_dec_kernel, grid=(1,), in_specs=[full(a) for a in args],
        out_specs=pl.BlockSpec((b, D_MODEL), lambda i: (0, 0)),
        out_shape=jax.ShapeDtypeStruct((b, D_MODEL), F32),
        compiler_params=_cparams(1), name="mix_dec",
    )(*args)


def _hyperbola_cells():
    return [(i, j) for i in range(PEER_TOPK) for j in range(PEER_TOPK) if (i + 1) * (j + 1) <= PEER_TOPK]


def _route_kernel(h_ref, g_ref, wq_ref, keys_ref, xn_ref, nc1_ref, r2_ref, e1_ref, e2_ref,
                  sc_s, rank_s, top_s, cnt_s, z_s):
    tm = h_ref.shape[0]
    nk = PEER_NKEYS
    xn = _rmsnorm(h_ref[...], g_ref[...]).astype(BF16)
    xn_ref[...] = xn
    qb = jnp.dot(xn, wq_ref[...], preferred_element_type=F32).astype(BF16)
    for hc in range(2 * PEER_HEADS):
        sc_s[hc] = _dot_nt(keys_ref[hc], qb[:, hc * nk:(hc + 1) * nk])

    key_f = lax.broadcasted_iota(jnp.int32, (nk, LANES), 0).astype(F32)
    sub = lax.broadcasted_iota(jnp.int32, (PEER_HEADS, LANES), 0)
    cells = _hyperbola_cells()
    for half in range(tm // LANES):
        cols = slice(half * LANES, (half + 1) * LANES)

        def rank_body(hc, carry):
            x = sc_s[hc, :, cols]
            rank = jnp.full((nk, LANES), float(PEER_TOPK), F32)
            for r in range(PEER_TOPK):
                mx, hit = _first_max_onehot(x, key_f, 0)
                rank = jnp.where(hit, float(r), rank)
                x = jnp.where(hit, -jnp.inf, x)
                top_s[hc, r, :, cols] = jnp.broadcast_to(mx, (8, LANES))
            rank_s[hc, :, cols] = rank
            return carry

        lax.fori_loop(0, 2 * PEER_HEADS, rank_body, 0)

        def heads_on_sublanes(side, r):
            out = top_s[side, r, :, cols]
            for h in range(1, PEER_HEADS):
                out = jnp.where(sub == h, top_s[2 * h + side, r, :, cols], out)
            return out

        a = [heads_on_sublanes(0, r) for r in range(PEER_TOPK)]
        bb = [heads_on_sublanes(1, r) for r in range(PEER_TOPK)]
        cur0 = tuple(a[i] + bb[j] for (i, j) in cells)
        best = cur0[0]

        def pick_body(_, carry):
            cur, taken = carry
            mx = functools.reduce(jnp.maximum, cur)
            found = jnp.zeros_like(mx)
            new_cur, new_taken = [], []
            for ci in range(len(cells)):
                hit = jnp.where(cur[ci] == mx, 1.0 - found, 0.0)
                found = found + hit
                new_taken.append(taken[ci] + hit)
                new_cur.append(jnp.where(hit > 0.0, -jnp.inf, cur[ci]))
            return tuple(new_cur), tuple(new_taken)

        _, taken = lax.fori_loop(0, PEER_TOPK, pick_body,
                                 (cur0, tuple(jnp.zeros_like(best) for _ in cells)))
        z = jnp.zeros_like(best)
        counts = [jnp.zeros_like(best) for _ in range(PEER_TOPK)]
        for ci, (i, j) in enumerate(cells):
            counts[i] = counts[i] + taken[ci]
            z = z + taken[ci] * jnp.exp(a[i] + bb[j] - best)
        for h in range(PEER_HEADS):
            z_s[h, :, cols] = jnp.broadcast_to(z[h:h + 1], (8, LANES))
            for r in range(PEER_TOPK):
                cnt_s[h, r, :, cols] = jnp.broadcast_to(counts[r][h:h + 1], (8, LANES))

        def gate_body(h, carry):
            r1 = rank_s[2 * h, :, cols]
            nc = jnp.zeros((nk, LANES), F32)
            for r in range(PEER_TOPK):
                nc = jnp.where(r1 == float(r), cnt_s[h, r, 0:1, cols], nc)
            nc1_ref[h, :, cols] = nc
            r2_ref[h, :, cols] = rank_s[2 * h + 1, :, cols]
            a0 = top_s[2 * h, 0, 0:1, cols]
            b0 = top_s[2 * h + 1, 0, 0:1, cols]
            e1_ref[h, :, cols] = jnp.exp(sc_s[2 * h, :, cols] - a0) / z_s[h, 0:1, cols]
            e2_ref[h, :, cols] = jnp.exp(sc_s[2 * h + 1, :, cols] - b0)
            return carry

        lax.fori_loop(0, PEER_HEADS, gate_body, 0)


def _peer_route(h1, g_ffn, wq_bf, keys_bf, *, tm):
    t = h1.shape[0]
    assert t % tm == 0 and tm % LANES == 0
    const = lambda shape: pl.BlockSpec(shape, lambda i: (0,) * len(shape))
    gate_shape = jax.ShapeDtypeStruct((PEER_HEADS, PEER_NKEYS, t), F32)
    gate_spec = pl.BlockSpec((PEER_HEADS, PEER_NKEYS, tm), lambda i: (0, 0, i))
    return pl.pallas_call(
        _route_kernel,
        grid=(t // tm,),
        in_specs=[pl.BlockSpec((tm, D_MODEL), lambda i: (i, 0)), const((1, D_MODEL)),
                  const(wq_bf.shape), const(keys_bf.shape)],
        out_specs=[pl.BlockSpec((tm, D_MODEL), lambda i: (i, 0))] + [gate_spec] * 4,
        out_shape=[jax.ShapeDtypeStruct((t, D_MODEL), BF16)] + [gate_shape] * 4,
        scratch_shapes=[pltpu.VMEM((2 * PEER_HEADS, PEER_NKEYS, tm), F32),
                        pltpu.VMEM((2 * PEER_HEADS, PEER_NKEYS, tm), F32),
                        pltpu.VMEM((2 * PEER_HEADS, PEER_TOPK, 8, tm), F32),
                        pltpu.VMEM((PEER_HEADS, PEER_TOPK, 8, tm), F32),
                        pltpu.VMEM((PEER_HEADS, 8, tm), F32)],
        compiler_params=_cparams(1), name="route",
    )(h1, g_ffn, wq_bf, keys_bf)


def _experts_kernel(xn_ref, u_ref, vt_ref, nc1_ref, r2_ref, e1_ref, e2_ref, h_ref, o_ref, acc):
    c = pl.program_id(1)

    @pl.when(c == 0)
    def _():
        acc[...] = jnp.zeros_like(acc)

    pre = _dot_nt(u_ref[...], xn_ref[...])
    act = 0.5 * pre * (1.0 + lax.erf(pre * (2.0 ** -0.5)))
    halves = []
    for u in range(EXPERT_BLOCK // PEER_NKEYS):
        i1 = c * (EXPERT_BLOCK // PEER_NKEYS) + u
        g = jnp.zeros((PEER_NKEYS, act.shape[1]), F32)
        for h in range(PEER_HEADS):
            nc = nc1_ref[h, pl.ds(i1, 1), :]
            e1 = e1_ref[h, pl.ds(i1, 1), :]
            g = g + jnp.where(r2_ref[h] < nc, e2_ref[h], 0.0) * e1
        halves.append(g * act[u * PEER_NKEYS:(u + 1) * PEER_NKEYS])
    w = jnp.concatenate(halves, axis=0).astype(BF16)
    acc[...] += jnp.dot(vt_ref[...], w, preferred_element_type=F32)

    @pl.when(c == pl.num_programs(1) - 1)
    def _():
        o_ref[...] = h_ref[...] + acc[...].T


def _peer_experts(xn, u_bf, vt_bf, nc1, r2, e1, e2, h1, *, tm):
    t = xn.shape[0]
    n_exp = u_bf.shape[0]
    assert t % tm == 0 and n_exp % EXPERT_BLOCK == 0
    gate_spec = pl.BlockSpec((PEER_HEADS, PEER_NKEYS, tm), lambda i, c: (0, 0, i))
    tok_spec = pl.BlockSpec((tm, D_MODEL), lambda i, c: (i, 0))
    return pl.pallas_call(
        _experts_kernel,
        grid=(t // tm, n_exp // EXPERT_BLOCK),
        in_specs=[tok_spec,
                  pl.BlockSpec((EXPERT_BLOCK, D_MODEL), lambda i, c: (c, 0)),
                  pl.BlockSpec((D_MODEL, EXPERT_BLOCK), lambda i, c: (0, c)),
                  gate_spec, gate_spec, gate_spec, gate_spec, tok_spec],
        out_specs=tok_spec,
        out_shape=jax.ShapeDtypeStruct((t, D_MODEL), F32),
        scratch_shapes=[pltpu.VMEM((D_MODEL, tm), F32)],
        compiler_params=_cparams(2), name="experts",
    )(xn, u_bf, vt_bf, nc1, r2, e1, e2, h1)


def _ple_kernel(h_ref, p_ref, gple_ref, wg_ref, wp_ref, gfin_ref, y_ref):
    h = h_ref[...]
    gate = _sigmoid(jnp.dot(_rmsnorm(h, gple_ref[...]).astype(BF16), wg_ref[...], preferred_element_type=F32))
    proj = jnp.dot(p_ref[...].astype(BF16), wp_ref[...], preferred_element_type=F32)
    y_ref[...] = _rmsnorm(h + gate * proj, gfin_ref[...])


def _ple_final(h2, p_emb, g_ple, wg_bf, wp_bf, g_final, *, tm):
    t = h2.shape[0]
    const = lambda shape: pl.BlockSpec(shape, lambda i: (0,) * len(shape))
    return pl.pallas_call(
        _ple_kernel,
        grid=(t // tm,),
        in_specs=[pl.BlockSpec((tm, D_MODEL), lambda i: (i, 0)), pl.BlockSpec((tm, PLE_DIM), lambda i: (i, 0)),
                  const((1, D_MODEL)), const((D_MODEL, D_MODEL)), const((PLE_DIM, D_MODEL)), const((1, D_MODEL))],
        out_specs=pl.BlockSpec((tm, D_MODEL), lambda i: (i, 0)),
        out_shape=jax.ShapeDtypeStruct((t, D_MODEL), F32),
        compiler_params=_cparams(1), name="ple",
    )(h2, p_emb, g_ple, wg_bf, wp_bf, g_final)


def _pick_tile(t, want):
    tm = min(want, t)
    assert t % tm == 0
    return tm


def _ffn(h1, p_emb, lw, g_final):
    t = h1.shape[0]
    xn, nc1, r2, e1, e2 = _peer_route(h1, lw["g_ffn"], lw["wq"], lw["keys"], tm=_pick_tile(t, 256))
    h2 = _peer_experts(xn, lw["u"], lw["vt"], nc1, r2, e1, e2, h1, tm=_pick_tile(t, 512))
    return _ple_final(h2, p_emb, lw["g_ple"], lw["w_ple_gate"], lw["w_ple_proj"], g_final, tm=_pick_tile(t, 256))


def kernel(x_prompt, x_sample, cache_k, cache_v, cache_conv, page_table, p_prompt, p_sample,
           g_mix, w_in, w_dw, b_dw, ln_g, ln_b, w_out, g_ffn, peer_wq, peer_keys, peer_u, peer_v,
           g_ple, w_ple_gate, w_ple_proj, g_final):
    depth = g_mix.shape[0]
    batch, seq, _ = x_prompt.shape
    dec_batch, dec_seq, _ = x_sample.shape
    assert depth == 1 and batch == 1 and dec_seq == 1
    past_len = page_table.shape[1] * PAGE_SIZE
    assert seq % MOBA_BLOCK == 0 and past_len % MOBA_BLOCK == 0 and seq >= CONV_K - 1
    row = lambda a: a.reshape(1, -1)
    lane = jnp.arange(LANES) % (ROT_DIM // 2)
    inv_lane = (ROPE_THETA ** (-lane.astype(F32) * 2.0 / ROT_DIM)).reshape(1, LANES)
    l = 0
    lw = dict(
        g_ffn=row(g_ffn[l]), wq=peer_wq[l].astype(BF16),
        keys=peer_keys[l].reshape(2 * PEER_HEADS, PEER_NKEYS, -1).astype(BF16),
        u=peer_u[l].astype(BF16), vt=peer_v[l].T.astype(BF16),
        g_ple=row(g_ple[l]), w_ple_gate=w_ple_gate[l].astype(BF16), w_ple_proj=w_ple_proj[l].astype(BF16))
    w_in_bf = w_in[l].astype(BF16)
    w_out_bf = w_out[l].astype(BF16)
    w_dw_pad = jnp.pad(w_dw[l], ((0, CONV_HALO - CONV_K), (0, 0)))
    conv_args = (w_dw_pad, row(b_dw[l]), row(ln_g[l]), row(ln_b[l]), w_out_bf)
    g_fin = row(g_final)

    xp = x_prompt.reshape(seq, D_MODEL)
    k_p, v_p, glu_p, qp, kp, vt, kmean = _inproj(xp, row(g_mix[l]), w_in_bf, inv_lane, pos0=0, pos_stride=1,
                                                 tm=MOBA_BLOCK, with_aux=True)
    attn_p = _moba_prompt(qp, kp, vt, kmean)
    h1_p = _mix_prompt(glu_p, attn_p, xp, *conv_args, tm=_pick_tile(seq, 256))
    y_p = _ffn(h1_p, p_prompt[l].reshape(seq, PLE_DIM), lw, g_fin)

    xs = x_sample.reshape(dec_batch, D_MODEL)
    k_s, v_s, glu_s, q_s = _inproj(xs, row(g_mix[l]), w_in_bf, inv_lane, pos0=past_len, pos_stride=0,
                                   tm=dec_batch, with_aux=False)
    attn_s = _moba_decode(q_s, k_s, v_s, cache_k[l], cache_v[l], page_table)
    xin_s = jnp.concatenate([cache_conv[l], glu_s[:, None, :]], axis=1)
    xin_pad = jnp.pad(xin_s, ((0, 0), (0, CONV_HALO - CONV_K), (0, 0)))
    h1_s = _mix_decode(xin_pad, attn_s, xs, *conv_args)
    pad = (-dec_batch) % LANES
    h1_s = jnp.pad(h1_s, ((0, pad), (0, 0)))
    p_s = jnp.pad(p_sample[l].reshape(dec_batch, PLE_DIM), ((0, pad), (0, 0)))
    y_s = _ffn(h1_s, p_s, lw, g_fin)[:dec_batch]

    conv_p = glu_p[seq - (CONV_K - 1):].reshape(1, 1, CONV_K - 1, CONV_CH)
    conv_s = xin_s[:, 1:].reshape(1, dec_batch, CONV_K - 1, CONV_CH)
    return (y_p.reshape(1, seq, D_MODEL), y_s.reshape(dec_batch, 1, D_MODEL),
            k_p.reshape(1, 1, seq, N_HEADS, HEAD_DIM), v_p.reshape(1, 1, seq, N_HEADS, HEAD_DIM), conv_p,
            k_s.reshape(1, dec_batch, 1, N_HEADS, HEAD_DIM), v_s.reshape(1, dec_batch, 1, N_HEADS, HEAD_DIM),
            conv_s)
```

```python
import functools

import jax
import jax.numpy as jnp
from jax import lax
from jax.experimental import pallas as pl
from jax.experimental.pallas import tpu as pltpu

F32 = jnp.float32
BF16 = jnp.bfloat16

D_MODEL = 1024
N_HEADS = 8
HEAD_DIM = 64
ATT_WIDTH = N_HEADS * HEAD_DIM
CONV_CH = D_MODEL - ATT_WIDTH
ROT_DIM = HEAD_DIM // 4
ROPE_THETA = 500000.0
MOBA_BLOCK = 256
MOBA_TOPK = 3
CONV_K = 31
PAGE_SIZE = 128
PEER_HEADS = 8
PEER_NKEYS = 128
PEER_TOPK = 16
PLE_DIM = 256
NORM_EPS = 1e-6

LANES = 128
CONV_HALO = 32
MASKED = -1e30
EXPERT_BLOCK = 2 * PEER_NKEYS
VMEM_LIMIT = 56 * 1024 * 1024


def _cparams(n_axes):
    return pltpu.CompilerParams(dimension_semantics=("arbitrary",) * n_axes, vmem_limit_bytes=VMEM_LIMIT)


def _rmsnorm(x, g):
    return x * lax.rsqrt(jnp.mean(x * x, axis=-1, keepdims=True) + NORM_EPS) * g


def _sigmoid(x):
    return 1.0 / (1.0 + jnp.exp(-x))


def _dot_nt(a, b):
    return lax.dot_general(a, b, (((1,), (1,)), ((), ())), preferred_element_type=F32)


def _rope(x, cos4, sin4, low4):
    half = ROT_DIM // 2
    from_hi = pltpu.roll(x, ATT_WIDTH - half, 1)
    from_lo = pltpu.roll(x, half, 1)
    return x * cos4 + jnp.where(low4, from_hi, from_lo) * sin4


def _inproj_kernel(pos0, pos_stride, with_aux, x_ref, g_ref, w_ref, inv_ref, *out_refs):
    if with_aux:
        k_ref, v_ref, glu_ref, q_ref, kp_ref, vt_ref, kmean_ref = out_refs
    else:
        k_ref, v_ref, glu_ref, q_ref = out_refs
    tm = x_ref.shape[0]
    xn = _rmsnorm(x_ref[...], g_ref[...]).astype(BF16)
    z = jnp.dot(xn, w_ref[...], preferred_element_type=F32)
    q = z[:, 0 * ATT_WIDTH:1 * ATT_WIDTH]
    k = z[:, 1 * ATT_WIDTH:2 * ATT_WIDTH]
    v = z[:, 2 * ATT_WIDTH:3 * ATT_WIDTH]
    ga = z[:, 3 * ATT_WIDTH:3 * ATT_WIDTH + CONV_CH]
    gb = z[:, 3 * ATT_WIDTH + CONV_CH:]

    row = lax.broadcasted_iota(jnp.int32, (tm, LANES), 0)
    d = lax.broadcasted_iota(jnp.int32, (tm, LANES), 1) % HEAD_DIM
    pos = (pos0 + (pl.program_id(0) * tm + row) * pos_stride).astype(F32)
    ang = pos * inv_ref[...]
    cos = jnp.where(d < ROT_DIM, jnp.cos(ang), 1.0)
    sin = jnp.sin(ang)
    sin = jnp.where(d < ROT_DIM // 2, -sin, jnp.where(d < ROT_DIM, sin, 0.0))
    reps = ATT_WIDTH // LANES
    cos4 = jnp.concatenate([cos] * reps, axis=1)
    sin4 = jnp.concatenate([sin] * reps, axis=1)
    low4 = lax.broadcasted_iota(jnp.int32, (tm, ATT_WIDTH), 1) % HEAD_DIM < ROT_DIM // 2

    k = _rope(k, cos4, sin4, low4)
    q = _rope(q, cos4, sin4, low4) * (HEAD_DIM ** -0.5)
    k_ref[...] = k
    v_ref[...] = v
    glu_ref[...] = ga * _sigmoid(gb)
    if with_aux:
        q_ref[...] = q.astype(BF16)
        kp_ref[...] = k.astype(BF16)
        kmean_ref[0] = jnp.mean(k, axis=0, keepdims=True)
        vt = v.T
        for h in range(N_HEADS):
            vt_ref[h, 0] = vt[h * HEAD_DIM:(h + 1) * HEAD_DIM, :].astype(BF16)
    else:
        q_ref[...] = q


def _inproj(x, g_mix, w_in_bf, inv_lane, *, pos0, pos_stride, tm, with_aux):
    t = x.shape[0]
    n_in = w_in_bf.shape[1]
    grid = (t // tm,)
    row_spec = lambda w: pl.BlockSpec((tm, w), lambda i: (i, 0))
    const = lambda shape: pl.BlockSpec(shape, lambda i: (0,) * len(shape))
    out_shape = [jax.ShapeDtypeStruct((t, ATT_WIDTH), F32), jax.ShapeDtypeStruct((t, ATT_WIDTH), F32),
                 jax.ShapeDtypeStruct((t, CONV_CH), F32)]
    out_specs = [row_spec(ATT_WIDTH), row_spec(ATT_WIDTH), row_spec(CONV_CH)]
    if with_aux:
        assert tm == MOBA_BLOCK
        nb = t // MOBA_BLOCK
        out_shape += [jax.ShapeDtypeStruct((t, ATT_WIDTH), BF16), jax.ShapeDtypeStruct((t, ATT_WIDTH), BF16),
                      jax.ShapeDtypeStruct((N_HEADS, nb, HEAD_DIM, MOBA_BLOCK), BF16),
                      jax.ShapeDtypeStruct((nb, 1, ATT_WIDTH), F32)]
        out_specs += [row_spec(ATT_WIDTH), row_spec(ATT_WIDTH),
                      pl.BlockSpec((N_HEADS, 1, HEAD_DIM, MOBA_BLOCK), lambda i: (0, i, 0, 0)),
                      pl.BlockSpec((1, 1, ATT_WIDTH), lambda i: (i, 0, 0))]
    else:
        out_shape += [jax.ShapeDtypeStruct((t, ATT_WIDTH), F32)]
        out_specs += [row_spec(ATT_WIDTH)]
    return pl.pallas_call(
        functools.partial(_inproj_kernel, pos0, pos_stride, with_aux),
        grid=grid,
        in_specs=[row_spec(D_MODEL), const((1, D_MODEL)), const((D_MODEL, n_in)), const((1, LANES))],
        out_specs=out_specs, out_shape=out_shape, compiler_params=_cparams(1),
        name="inproj_aux" if with_aux else "inproj",
    )(x, g_mix, w_in_bf, inv_lane)


def _first_max_onehot(g, idx_f, axis):
    mx = jnp.max(g, axis=axis, keepdims=True)
    first = jnp.min(jnp.where(g == mx, idx_f, jnp.inf), axis=axis, keepdims=True)
    return mx, idx_f == first


def _moba_kernel(chunk, q_ref, kp_ref, vt_ref, kmean_ref, o_ref):
    i = pl.program_id(1)
    blk = MOBA_BLOCK
    q2 = q_ref[...]
    km = kmean_ref[...].astype(BF16)
    lane = lax.broadcasted_iota(jnp.int32, (blk, LANES), 1)
    lane_f = lane.astype(F32)
    key_row = lax.broadcasted_iota(jnp.int32, (blk, blk), 0)
    qry_col = lax.broadcasted_iota(jnp.int32, (blk, blk), 1)

    qzs, biases = [], []
    for hh in range(2):
        qz = jnp.where(lane // HEAD_DIM == hh, q2, jnp.zeros_like(q2))
        gate = _dot_nt(qz, km)
        g = jnp.where(lane < i, gate, -jnp.inf)
        sel = jnp.zeros((blk, LANES), jnp.bool_)
        for _ in range(MOBA_TOPK):
            mx, hit = _first_max_onehot(g, lane_f, 1)
            sel = sel | (hit & (mx > -jnp.inf))
            g = jnp.where(hit, -jnp.inf, g)
        qzs.append(qz)
        biases.append(jnp.where(sel, 0.0, MASKED))

    k_own = kp_ref[pl.ds(pl.multiple_of(i * blk, blk), blk), :]
    state = []
    for hh in range(2):
        s = _dot_nt(k_own, qzs[hh])
        s = jnp.where(key_row <= qry_col, s, MASKED)
        m = jnp.max(s, axis=0, keepdims=True)
        p = jnp.exp(s - m)
        l = jnp.sum(p, axis=0, keepdims=True)
        acc = jnp.dot(vt_ref[hh, i], p.astype(BF16), preferred_element_type=F32)
        state += [m, l, acc]

    def body(c, carry):
        base = c * chunk
        shift = lax.rem(LANES - base, LANES)
        k_augs = []
        for r in range(chunk):
            kr = kp_ref[pl.ds(pl.multiple_of((base + r) * blk, blk), blk), :]
            k_augs.append(jnp.concatenate([kr, (lane == r).astype(BF16)], axis=1))
        out = []
        for hh in range(2):
            m, l, acc = carry[3 * hh:3 * hh + 3]
            bias_c = pltpu.roll(biases[hh], shift, 1).astype(BF16)
            q_aug = jnp.concatenate([qzs[hh], bias_c], axis=1)
            ss = [_dot_nt(ka, q_aug) for ka in k_augs]
            mc = functools.reduce(jnp.maximum, [jnp.max(s, axis=0, keepdims=True) for s in ss])
            m_new = jnp.maximum(m, mc)
            alpha = jnp.exp(m - m_new)
            l = alpha * l
            acc = alpha * acc
            for r in range(chunk):
                p = jnp.exp(ss[r] - m_new)
                l = l + jnp.sum(p, axis=0, keepdims=True)
                acc = acc + jnp.dot(vt_ref[hh, base + r], p.astype(BF16), preferred_element_type=F32)
            out += [m_new, l, acc]
        return tuple(out)

    state = lax.fori_loop(0, (i + chunk - 1) // chunk, body, tuple(state))
    o_t = jnp.concatenate([state[2] / state[1], state[5] / state[4]], axis=0)
    o_ref[...] = o_t.T.astype(o_ref.dtype)


def _moba_prompt(qp, kp, vt, kmean):
    t = qp.shape[0]
    nb = t // MOBA_BLOCK
    assert t % MOBA_BLOCK == 0 and nb <= LANES
    chunk = next(c for c in (8, 4, 2, 1) if nb % c == 0)
    km = jnp.pad(kmean.reshape(nb, ATT_WIDTH), ((0, LANES - nb), (0, 0)))
    return pl.pallas_call(
        functools.partial(_moba_kernel, chunk),
        grid=(N_HEADS // 2, nb),
        in_specs=[pl.BlockSpec((MOBA_BLOCK, LANES), lambda hp, i: (i, hp)),
                  pl.BlockSpec((t, LANES), lambda hp, i: (0, hp)),
                  pl.BlockSpec((2, nb, HEAD_DIM, MOBA_BLOCK), lambda hp, i: (hp, 0, 0, 0)),
                  pl.BlockSpec((LANES, LANES), lambda hp, i: (0, hp))],
        out_specs=pl.BlockSpec((MOBA_BLOCK, LANES), lambda hp, i: (i, hp)),
        out_shape=jax.ShapeDtypeStruct((t, ATT_WIDTH), BF16),
        compiler_params=_cparams(2), name="moba",
    )(qp, kp, vt, km)


def _moba_dec_kernel(bps, pt_ref, q_ref, qb_ref, kn_ref, vn_ref, *refs):
    del pt_ref
    k_refs, v_refs = refs[:2 * bps], refs[2 * bps:4 * bps]
    o_ref, gate_s, m_s, l_s, o_s = refs[4 * bps:]
    step = pl.program_id(1)
    blk_lane = lax.broadcasted_iota(jnp.int32, (N_HEADS, LANES), 1)

    @pl.when(step == 0)
    def _():
        gate_s[...] = jnp.full_like(gate_s, -jnp.inf)
        m_s[...] = jnp.zeros_like(m_s)
        l_s[...] = jnp.zeros_like(l_s)

    qb = qb_ref[0]
    for bi in range(bps):
        n = step * bps + bi
        s = jnp.concatenate([jnp.sum(k_refs[2 * bi + w][0] * qb, axis=1) for w in range(2)], axis=1)
        m = jnp.max(s, axis=1, keepdims=True)
        p = jnp.exp(s - m)
        o = jnp.zeros((N_HEADS, HEAD_DIM), F32)
        for w in range(2):
            pw = p[:, w * PAGE_SIZE:(w + 1) * PAGE_SIZE]
            o = o + jnp.sum(v_refs[2 * bi + w][0] * pw[:, None, :], axis=2)
        o_s[n] = o
        here = blk_lane == n
        gate_s[...] = jnp.where(here, jnp.mean(s, axis=1, keepdims=True), gate_s[...])
        m_s[...] = jnp.where(here, m, m_s[...])
        l_s[...] = jnp.where(here, jnp.sum(p, axis=1, keepdims=True), l_s[...])

    @pl.when(step == pl.num_programs(1) - 1)
    def _():
        g = gate_s[...]
        sel = jnp.zeros((N_HEADS, LANES), jnp.bool_)
        for _ in range(MOBA_TOPK):
            mx, hit = _first_max_onehot(g, blk_lane.astype(F32), 1)
            sel = sel | (hit & (mx > -jnp.inf))
            g = jnp.where(hit, -jnp.inf, g)
        s_own = jnp.sum(kn_ref[0] * q_ref[0], axis=1, keepdims=True)
        mm = m_s[...]
        big = jnp.maximum(jnp.max(jnp.where(sel, mm, MASKED), axis=1, keepdims=True), s_own)
        w = jnp.where(sel, jnp.exp(mm - big), 0.0)
        w_own = jnp.exp(s_own - big)
        den = jnp.sum(w * l_s[...], axis=1, keepdims=True) + w_own
        num = w_own * vn_ref[0]
        for nn in range(o_s.shape[0]):
            num = num + w[:, nn:nn + 1] * o_s[nn]
        o_ref[0] = num / den


def _moba_decode(q_s, k_new, v_new, cache_k, cache_v, page_table):
    b = q_s.shape[0]
    n_pages = page_table.shape[1]
    pages_per_blk = MOBA_BLOCK // PAGE_SIZE
    assert pages_per_blk == 2 and n_pages % pages_per_blk == 0
    nbp = n_pages // pages_per_blk
    assert nbp <= LANES
    bps = 2 if nbp % 2 == 0 else 1
    ck = jnp.transpose(cache_k, (0, 2, 3, 1))
    cv = jnp.transpose(cache_v, (0, 2, 3, 1))
    heads = lambda a: a.reshape(b, N_HEADS, HEAD_DIM)
    q_lanes = jnp.broadcast_to(heads(q_s)[..., None], (b, N_HEADS, HEAD_DIM, LANES))
    row_spec = pl.BlockSpec((1, N_HEADS, HEAD_DIM), lambda s, n, pt: (s, 0, 0))
    page = lambda j: pl.BlockSpec((1, N_HEADS, HEAD_DIM, PAGE_SIZE),
                                  lambda s, n, pt: (pt[s, pages_per_blk * bps * n + j], 0, 0, 0))
    pages = [page(j) for j in range(pages_per_blk * bps)]
    out = pl.pallas_call(
        functools.partial(_moba_dec_kernel, bps),
        grid_spec=pltpu.PrefetchScalarGridSpec(
            num_scalar_prefetch=1, grid=(b, nbp // bps),
            in_specs=[row_spec, pl.BlockSpec((1, N_HEADS, HEAD_DIM, LANES), lambda s, n, pt: (s, 0, 0, 0)),
                      row_spec, row_spec] + pages + pages,
            out_specs=row_spec,
            scratch_shapes=[pltpu.VMEM((N_HEADS, LANES), F32), pltpu.VMEM((N_HEADS, LANES), F32),
                            pltpu.VMEM((N_HEADS, LANES), F32), pltpu.VMEM((nbp, N_HEADS, HEAD_DIM), F32)]),
        out_shape=jax.ShapeDtypeStruct((b, N_HEADS, HEAD_DIM), F32),
        compiler_params=_cparams(2), name="moba_dec",
    )(page_table, heads(q_s), q_lanes, heads(k_new), heads(v_new), *([ck] * len(pages)), *([cv] * len(pages)))
    return out.reshape(b, ATT_WIDTH)


def _post_conv(y, attn_bf, x, b_dw, ln_g, ln_b, w_out):
    y = y + b_dw
    mu = jnp.mean(y, axis=-1, keepdims=True)
    yc = y - mu
    var = jnp.mean(yc * yc, axis=-1, keepdims=True)
    y = yc * lax.rsqrt(var + NORM_EPS) * ln_g + ln_b
    y = y * _sigmoid(y)
    cat = jnp.concatenate([attn_bf, y.astype(BF16)], axis=1)
    return x + jnp.dot(cat, w_out, preferred_element_type=F32)


def _mix_kernel(glu_ref, halo_ref, attn_ref, x_ref, wdw_ref, bdw_ref, lng_ref, lnb_ref, wout_ref, h_ref, xin):
    tm = glu_ref.shape[0]
    first = pl.program_id(0) == 0
    xin[0:CONV_HALO, :] = jnp.where(first, 0.0, halo_ref[...])
    xin[CONV_HALO:, :] = glu_ref[...]
    off = CONV_HALO - (CONV_K - 1)
    y = jnp.zeros((tm, CONV_CH), F32)
    for k in range(CONV_K):
        y = y + wdw_ref[k:k + 1, :] * xin[off + k:off + k + tm, :]
    h_ref[...] = _post_conv(y, attn_ref[...], x_ref[...], bdw_ref[...], lng_ref[...], lnb_ref[...], wout_ref[...])


def _mix_prompt(glu, attn_bf, x, w_dw, b_dw, ln_g, ln_b, w_out_bf, *, tm):
    t = glu.shape[0]
    assert tm % CONV_HALO == 0 and t % tm == 0
    ratio = tm // CONV_HALO
    const = lambda shape: pl.BlockSpec(shape, lambda i: (0,) * len(shape))
    return pl.pallas_call(
        _mix_kernel,
        grid=(t // tm,),
        in_specs=[pl.BlockSpec((tm, CONV_CH), lambda i: (i, 0)),
                  pl.BlockSpec((CONV_HALO, CONV_CH), lambda i: (jnp.maximum(i * ratio - 1, 0), 0)),
                  pl.BlockSpec((tm, ATT_WIDTH), lambda i: (i, 0)),
                  pl.BlockSpec((tm, D_MODEL), lambda i: (i, 0)),
                  const((CONV_HALO, CONV_CH)), const((1, CONV_CH)), const((1, CONV_CH)), const((1, CONV_CH)),
                  const((D_MODEL, D_MODEL))],
        out_specs=pl.BlockSpec((tm, D_MODEL), lambda i: (i, 0)),
        out_shape=jax.ShapeDtypeStruct((t, D_MODEL), F32),
        scratch_shapes=[pltpu.VMEM((tm + CONV_HALO, CONV_CH), F32)],
        compiler_params=_cparams(1), name="mix",
    )(glu, glu, attn_bf, x, w_dw, b_dw, ln_g, ln_b, w_out_bf)


def _mix_dec_kernel(xin_ref, attn_ref, x_ref, wdw_ref, bdw_ref, lng_ref, lnb_ref, wout_ref, h_ref):
    y = jnp.sum(xin_ref[...] * wdw_ref[...][None], axis=1)
    h_ref[...] = _post_conv(y, attn_ref[...].astype(BF16), x_ref[...], bdw_ref[...], lng_ref[...],
                            lnb_ref[...], wout_ref[...])


def _mix_decode(xin, attn, x, w_dw, b_dw, ln_g, ln_b, w_out_bf):
    b = x.shape[0]
    full = lambda a: pl.BlockSpec(a.shape, lambda i: (0,) * a.ndim)
    args = (xin, attn, x, w_dw, b_dw, ln_g, ln_b, w_out_bf)
    return pl.pallas_call(
        _mix_dec_kernel, grid=(1,), in_specs=[full(a) for a in args],
        out_specs=pl.BlockSpec((b, D_MODEL), lambda i: (0, 0)),
        out_shape=jax.ShapeDtypeStruct((b, D_MODEL), F32),
        compiler_params=_cparams(1), name="mix_dec",
    )(*args)


def _hyperbola_cells():
    return [(i, j) for i in range(PEER_TOPK) for j in range(PEER_TOPK) if (i + 1) * (j + 1) <= PEER_TOPK]


def _route_kernel(h_ref, g_ref, wq_ref, keys_ref, xn_ref, nc1_ref, r2_ref, e1_ref, e2_ref,
                  sc_s, rank_s, top_s, cnt_s, z_s):
    tm = h_ref.shape[0]
    nk = PEER_NKEYS
    xn = _rmsnorm(h_ref[...], g_ref[...]).astype(BF16)
    xn_ref[...] = xn
    qb = jnp.dot(xn, wq_ref[...], preferred_element_type=F32).astype(BF16)
    for hc in range(2 * PEER_HEADS):
        sc_s[hc] = _dot_nt(keys_ref[hc], qb[:, hc * nk:(hc + 1) * nk])

    key_f = lax.broadcasted_iota(jnp.int32, (nk, tm), 0).astype(F32)
    sub = lax.broadcasted_iota(jnp.int32, (PEER_HEADS, tm), 0)
    cells = _hyperbola_cells()

    def rank_body(hc, carry):
        x = sc_s[hc]
        rank = jnp.full((nk, tm), float(PEER_TOPK), F32)
        for r in range(PEER_TOPK):
            mx, hit = _first_max_onehot(x, key_f, 0)
            rank = jnp.where(hit, float(r), rank)
            x = jnp.where(hit, -jnp.inf, x)
            top_s[hc, r] = jnp.broadcast_to(mx, (8, tm))
        rank_s[hc] = rank
        return carry

    lax.fori_loop(0, 2 * PEER_HEADS, rank_body, 0)

    def heads_on_sublanes(side, r):
        out = top_s[side, r]
        for h in range(1, PEER_HEADS):
            out = jnp.where(sub == h, top_s[2 * h + side, r], out)
        return out

    a = [heads_on_sublanes(0, r) for r in range(PEER_TOPK)]
    bb = [heads_on_sublanes(1, r) for r in range(PEER_TOPK)]
    cur0 = tuple(a[i] + bb[j] for (i, j) in cells)
    best = cur0[0]

    def pick_body(_, carry):
        cur, taken = carry
        mx = functools.reduce(jnp.maximum, cur)
        first = functools.reduce(jnp.minimum, [jnp.where(cur[ci] == mx, float(ci), jnp.inf)
                                               for ci in range(len(cells))])
        hits = [first == float(ci) for ci in range(len(cells))]
        return (tuple(jnp.where(hit, -jnp.inf, c) for hit, c in zip(hits, cur)),
                tuple(jnp.where(hit, 1.0, t) for hit, t in zip(hits, taken)))

    _, taken = lax.fori_loop(0, PEER_TOPK, pick_body, (cur0, tuple(jnp.zeros_like(best) for _ in cells)))
    z = jnp.zeros_like(best)
    counts = [jnp.zeros_like(best) for _ in range(PEER_TOPK)]
    for ci, (i, j) in enumerate(cells):
        counts[i] = counts[i] + taken[ci]
        z = z + taken[ci] * jnp.exp(a[i] + bb[j] - best)
    for h in range(PEER_HEADS):
        z_s[h] = jnp.broadcast_to(z[h:h + 1], (8, tm))
        for r in range(PEER_TOPK):
            cnt_s[h, r] = jnp.broadcast_to(counts[r][h:h + 1], (8, tm))

    def gate_body(h, carry):
        r1 = rank_s[2 * h]
        nc = jnp.zeros((nk, tm), F32)
        for r in range(PEER_TOPK):
            nc = jnp.where(r1 == float(r), cnt_s[h, r, 0:1, :], nc)
        nc1_ref[h] = nc
        r2_ref[h] = rank_s[2 * h + 1].astype(r2_ref.dtype)
        a0 = top_s[2 * h, 0, 0:1, :]
        b0 = top_s[2 * h + 1, 0, 0:1, :]
        e1_ref[h] = jnp.exp(sc_s[2 * h] - a0) / z_s[h, 0:1, :]
        e2_ref[h] = jnp.exp(sc_s[2 * h + 1] - b0).astype(e2_ref.dtype)
        return carry

    lax.fori_loop(0, PEER_HEADS, gate_body, 0)


def _peer_route(h1, g_ffn, wq_bf, keys_bf, *, tm):
    t = h1.shape[0]
    assert t % tm == 0 and tm % LANES == 0
    const = lambda shape: pl.BlockSpec(shape, lambda i: (0,) * len(shape))
    gate_shape = lambda dt: jax.ShapeDtypeStruct((PEER_HEADS, PEER_NKEYS, t), dt)
    gate_spec = pl.BlockSpec((PEER_HEADS, PEER_NKEYS, tm), lambda i: (0, 0, i))
    return pl.pallas_call(
        _route_kernel,
        grid=(t // tm,),
        in_specs=[pl.BlockSpec((tm, D_MODEL), lambda i: (i, 0)), const((1, D_MODEL)),
                  const(wq_bf.shape), const(keys_bf.shape)],
        out_specs=[pl.BlockSpec((tm, D_MODEL), lambda i: (i, 0))] + [gate_spec] * 4,
        out_shape=[jax.ShapeDtypeStruct((t, D_MODEL), BF16),
                   gate_shape(F32), gate_shape(BF16), gate_shape(F32), gate_shape(BF16)],
        scratch_shapes=[pltpu.VMEM((2 * PEER_HEADS, PEER_NKEYS, tm), F32),
                        pltpu.VMEM((2 * PEER_HEADS, PEER_NKEYS, tm), F32),
                        pltpu.VMEM((2 * PEER_HEADS, PEER_TOPK, 8, tm), F32),
                        pltpu.VMEM((PEER_HEADS, PEER_TOPK, 8, tm), F32),
                        pltpu.VMEM((PEER_HEADS, 8, tm), F32)],
        compiler_params=_cparams(1), name="route",
    )(h1, g_ffn, wq_bf, keys_bf)


def _experts_kernel(xn_ref, u_ref, vt_ref, nc1_ref, r2_ref, e1_ref, e2_ref, h_ref, o_ref, acc):
    c = pl.program_id(1)

    @pl.when(c == 0)
    def _():
        acc[...] = jnp.zeros_like(acc)

    pre = _dot_nt(u_ref[...], xn_ref[...])
    act = (0.5 * pre * (1.0 + lax.erf(pre * (2.0 ** -0.5)))).astype(BF16)
    n_i1 = EXPERT_BLOCK // PEER_NKEYS
    zero = jnp.zeros((PEER_NKEYS, act.shape[1]), BF16)
    gates = [zero] * n_i1
    for h in range(PEER_HEADS):
        r2 = r2_ref[h]
        e2 = e2_ref[h]
        for u in range(n_i1):
            i1 = c * n_i1 + u
            nc = nc1_ref[h, pl.ds(i1, 1), :].astype(BF16)
            e1 = e1_ref[h, pl.ds(i1, 1), :].astype(BF16)
            gates[u] = gates[u] + jnp.where(r2 < nc, e2, zero) * e1
    w = jnp.concatenate(gates, axis=0) * act
    acc[...] += jnp.dot(vt_ref[...], w, preferred_element_type=F32)

    @pl.when(c == pl.num_programs(1) - 1)
    def _():
        o_ref[...] = h_ref[...] + acc[...].T


def _peer_experts(xn, u_bf, vt_bf, nc1, r2, e1, e2, h1, *, tm):
    t = xn.shape[0]
    n_exp = u_bf.shape[0]
    assert t % tm == 0 and n_exp % EXPERT_BLOCK == 0
    gate_spec = pl.BlockSpec((PEER_HEADS, PEER_NKEYS, tm), lambda i, c: (0, 0, i))
    tok_spec = pl.BlockSpec((tm, D_MODEL), lambda i, c: (i, 0))
    return pl.pallas_call(
        _experts_kernel,
        grid=(t // tm, n_exp // EXPERT_BLOCK),
        in_specs=[tok_spec,
                  pl.BlockSpec((EXPERT_BLOCK, D_MODEL), lambda i, c: (c, 0)),
                  pl.BlockSpec((D_MODEL, EXPERT_BLOCK), lambda i, c: (0, c)),
                  gate_spec, gate_spec, gate_spec, gate_spec, tok_spec],
        out_specs=tok_spec,
        out_shape=jax.ShapeDtypeStruct((t, D_MODEL), F32),
        scratch_shapes=[pltpu.VMEM((D_MODEL, tm), F32)],
        compiler_params=_cparams(2), name="experts",
    )(xn, u_bf, vt_bf, nc1, r2, e1, e2, h1)


def _ple_kernel(h_ref, p_ref, gple_ref, wg_ref, wp_ref, gfin_ref, y_ref):
    h = h_ref[...]
    gate = _sigmoid(jnp.dot(_rmsnorm(h, gple_ref[...]).astype(BF16), wg_ref[...], preferred_element_type=F32))
    proj = jnp.dot(p_ref[...].astype(BF16), wp_ref[...], preferred_element_type=F32)
    y_ref[...] = _rmsnorm(h + gate * proj, gfin_ref[...])


def _ple_final(h2, p_emb, g_ple, wg_bf, wp_bf, g_final, *, tm):
    t = h2.shape[0]
    const = lambda shape: pl.BlockSpec(shape, lambda i: (0,) * len(shape))
    return pl.pallas_call(
        _ple_kernel,
        grid=(t // tm,),
        in_specs=[pl.BlockSpec((tm, D_MODEL), lambda i: (i, 0)), pl.BlockSpec((tm, PLE_DIM), lambda i: (i, 0)),
                  const((1, D_MODEL)), const((D_MODEL, D_MODEL)), const((PLE_DIM, D_MODEL)), const((1, D_MODEL))],
        out_specs=pl.BlockSpec((tm, D_MODEL), lambda i: (i, 0)),
        out_shape=jax.ShapeDtypeStruct((t, D_MODEL), F32),
        compiler_params=_cparams(1), name="ple",
    )(h2, p_emb, g_ple, wg_bf, wp_bf, g_final)


def _pick_tile(t, want):
    tm = min(want, t)
    assert t % tm == 0
    return tm


def _ffn(h1, p_emb, lw, g_final):
    t = h1.shape[0]
    xn, nc1, r2, e1, e2 = _peer_route(h1, lw["g_ffn"], lw["wq"], lw["keys"], tm=_pick_tile(t, 256))
    h2 = _peer_experts(xn, lw["u"], lw["vt"], nc1, r2, e1, e2, h1, tm=_pick_tile(t, 512))
    return _ple_final(h2, p_emb, lw["g_ple"], lw["w_ple_gate"], lw["w_ple_proj"], g_final, tm=_pick_tile(t, 256))


def kernel(x_prompt, x_sample, cache_k, cache_v, cache_conv, page_table, p_prompt, p_sample,
           g_mix, w_in, w_dw, b_dw, ln_g, ln_b, w_out, g_ffn, peer_wq, peer_keys, peer_u, peer_v,
           g_ple, w_ple_gate, w_ple_proj, g_final):
    depth = g_mix.shape[0]
    batch, seq, _ = x_prompt.shape
    dec_batch, dec_seq, _ = x_sample.shape
    assert depth == 1 and batch == 1 and dec_seq == 1
    past_len = page_table.shape[1] * PAGE_SIZE
    assert seq % MOBA_BLOCK == 0 and past_len % MOBA_BLOCK == 0 and seq >= CONV_K - 1
    row = lambda a: a.reshape(1, -1)
    lane = jnp.arange(LANES) % (ROT_DIM // 2)
    inv_lane = (ROPE_THETA ** (-lane.astype(F32) * 2.0 / ROT_DIM)).reshape(1, LANES)
    l = 0
    lw = dict(
        g_ffn=row(g_ffn[l]), wq=peer_wq[l].astype(BF16),
        keys=peer_keys[l].reshape(2 * PEER_HEADS, PEER_NKEYS, -1).astype(BF16),
        u=peer_u[l].astype(BF16), vt=peer_v[l].T.astype(BF16),
        g_ple=row(g_ple[l]), w_ple_gate=w_ple_gate[l].astype(BF16), w_ple_proj=w_ple_proj[l].astype(BF16))
    w_in_bf = w_in[l].astype(BF16)
    w_out_bf = w_out[l].astype(BF16)
    w_dw_pad = jnp.pad(w_dw[l], ((0, CONV_HALO - CONV_K), (0, 0)))
    conv_args = (w_dw_pad, row(b_dw[l]), row(ln_g[l]), row(ln_b[l]), w_out_bf)
    g_fin = row(g_final)

    xp = x_prompt.reshape(seq, D_MODEL)
    k_p, v_p, glu_p, qp, kp, vt, kmean = _inproj(xp, row(g_mix[l]), w_in_bf, inv_lane, pos0=0, pos_stride=1,
                                                 tm=MOBA_BLOCK, with_aux=True)
    attn_p = _moba_prompt(qp, kp, vt, kmean)
    h1_p = _mix_prompt(glu_p, attn_p, xp, *conv_args, tm=_pick_tile(seq, 256))
    y_p = _ffn(h1_p, p_prompt[l].reshape(seq, PLE_DIM), lw, g_fin)

    xs = x_sample.reshape(dec_batch, D_MODEL)
    k_s, v_s, glu_s, q_s = _inproj(xs, row(g_mix[l]), w_in_bf, inv_lane, pos0=past_len, pos_stride=0,
                                   tm=dec_batch, with_aux=False)
    attn_s = _moba_decode(q_s, k_s, v_s, cache_k[l], cache_v[l], page_table)
    xin_s = jnp.concatenate([cache_conv[l], glu_s[:, None, :]], axis=1)
    xin_pad = jnp.pad(xin_s, ((0, 0), (0, CONV_HALO - CONV_K), (0, 0)))
    h1_s = _mix_decode(xin_pad, attn_s, xs, *conv_args)
    pad = (-dec_batch) % LANES
    h1_s = jnp.pad(h1_s, ((0, pad), (0, 0)))
    p_s = jnp.pad(p_sample[l].reshape(dec_batch, PLE_DIM), ((0, pad), (0, 0)))
    y_s = _ffn(h1_s, p_s, lw, g_fin)[:dec_batch]

    conv_p = glu_p[seq - (CONV_K - 1):].reshape(1, 1, CONV_K - 1, CONV_CH)
    conv_s = xin_s[:, 1:].reshape(1, dec_batch, CONV_K - 1, CONV_CH)
    return (y_p.reshape(1, seq, D_MODEL), y_s.reshape(dec_batch, 1, D_MODEL),
            k_p.reshape(1, 1, seq, N_HEADS, HEAD_DIM), v_p.reshape(1, 1, seq, N_HEADS, HEAD_DIM), conv_p,
            k_s.reshape(1, dec_batch, 1, N_HEADS, HEAD_DIM), v_s.reshape(1, dec_batch, 1, N_HEADS, HEAD_DIM),
            conv_s)
```

```python
import functools

import jax
import jax.numpy as jnp
from jax import lax
from jax.experimental import pallas as pl
from jax.experimental.pallas import tpu as pltpu

F32 = jnp.float32
BF16 = jnp.bfloat16

D_MODEL = 1024
N_HEADS = 8
HEAD_DIM = 64
ATT_WIDTH = N_HEADS * HEAD_DIM
CONV_CH = D_MODEL - ATT_WIDTH
ROT_DIM = HEAD_DIM // 4
ROPE_THETA = 500000.0
MOBA_BLOCK = 256
MOBA_TOPK = 3
CONV_K = 31
PAGE_SIZE = 128
PEER_HEADS = 8
PEER_NKEYS = 128
PEER_TOPK = 16
PLE_DIM = 256
NORM_EPS = 1e-6

LANES = 128
CONV_HALO = 32
MASKED = -1e30
EXPERT_BLOCK = 2 * PEER_NKEYS
VMEM_LIMIT = 56 * 1024 * 1024
LOG2_E = 1.4426950408889634


def _cparams(n_axes):
    return pltpu.CompilerParams(dimension_semantics=("arbitrary",) * n_axes, vmem_limit_bytes=VMEM_LIMIT)


def _rmsnorm(x, g):
    return x * lax.rsqrt(jnp.mean(x * x, axis=-1, keepdims=True) + NORM_EPS) * g


def _sigmoid(x):
    return 1.0 / (1.0 + jnp.exp(-x))


def _dot_nt(a, b):
    return lax.dot_general(a, b, (((1,), (1,)), ((), ())), preferred_element_type=F32)


def _rope(x, cos4, sin4, low4):
    half = ROT_DIM // 2
    from_hi = pltpu.roll(x, ATT_WIDTH - half, 1)
    from_lo = pltpu.roll(x, half, 1)
    return x * cos4 + jnp.where(low4, from_hi, from_lo) * sin4


def _inproj_kernel(pos0, pos_stride, with_aux, x_ref, g_ref, w_ref, inv_ref, *out_refs):
    if with_aux:
        k_ref, v_ref, glu_ref, q_ref, kp_ref, vt_ref, kmean_ref = out_refs
    else:
        k_ref, v_ref, glu_ref, q_ref = out_refs
    tm = x_ref.shape[0]
    xn = _rmsnorm(x_ref[...], g_ref[...]).astype(BF16)
    z = jnp.dot(xn, w_ref[...], preferred_element_type=F32)
    q = z[:, 0 * ATT_WIDTH:1 * ATT_WIDTH]
    k = z[:, 1 * ATT_WIDTH:2 * ATT_WIDTH]
    v = z[:, 2 * ATT_WIDTH:3 * ATT_WIDTH]
    ga = z[:, 3 * ATT_WIDTH:3 * ATT_WIDTH + CONV_CH]
    gb = z[:, 3 * ATT_WIDTH + CONV_CH:]

    row = lax.broadcasted_iota(jnp.int32, (tm, LANES), 0)
    d = lax.broadcasted_iota(jnp.int32, (tm, LANES), 1) % HEAD_DIM
    pos = (pos0 + (pl.program_id(0) * tm + row) * pos_stride).astype(F32)
    ang = pos * inv_ref[...]
    cos = jnp.where(d < ROT_DIM, jnp.cos(ang), 1.0)
    sin = jnp.sin(ang)
    sin = jnp.where(d < ROT_DIM // 2, -sin, jnp.where(d < ROT_DIM, sin, 0.0))
    reps = ATT_WIDTH // LANES
    cos4 = jnp.concatenate([cos] * reps, axis=1)
    sin4 = jnp.concatenate([sin] * reps, axis=1)
    low4 = lax.broadcasted_iota(jnp.int32, (tm, ATT_WIDTH), 1) % HEAD_DIM < ROT_DIM // 2

    k = _rope(k, cos4, sin4, low4)
    q = _rope(q, cos4, sin4, low4) * (HEAD_DIM ** -0.5)
    k_ref[...] = k
    v_ref[...] = v
    glu_ref[...] = ga * _sigmoid(gb)
    if with_aux:
        q_ref[...] = (q * LOG2_E).astype(BF16)
        kp_ref[...] = k.astype(BF16)
        kmean_ref[0] = jnp.mean(k, axis=0, keepdims=True)
        vt = v.T
        for h in range(N_HEADS):
            vt_ref[h, 0] = vt[h * HEAD_DIM:(h + 1) * HEAD_DIM, :].astype(BF16)
    else:
        q_ref[...] = q


def _inproj(x, g_mix, w_in_bf, inv_lane, *, pos0, pos_stride, tm, with_aux):
    t = x.shape[0]
    n_in = w_in_bf.shape[1]
    grid = (t // tm,)
    row_spec = lambda w: pl.BlockSpec((tm, w), lambda i: (i, 0))
    const = lambda shape: pl.BlockSpec(shape, lambda i: (0,) * len(shape))
    out_shape = [jax.ShapeDtypeStruct((t, ATT_WIDTH), F32), jax.ShapeDtypeStruct((t, ATT_WIDTH), F32),
                 jax.ShapeDtypeStruct((t, CONV_CH), F32)]
    out_specs = [row_spec(ATT_WIDTH), row_spec(ATT_WIDTH), row_spec(CONV_CH)]
    if with_aux:
        assert tm == MOBA_BLOCK
        nb = t // MOBA_BLOCK
        out_shape += [jax.ShapeDtypeStruct((t, ATT_WIDTH), BF16), jax.ShapeDtypeStruct((t, ATT_WIDTH), BF16),
                      jax.ShapeDtypeStruct((N_HEADS, nb, HEAD_DIM, MOBA_BLOCK), BF16),
                      jax.ShapeDtypeStruct((nb, 1, ATT_WIDTH), F32)]
        out_specs += [row_spec(ATT_WIDTH), row_spec(ATT_WIDTH),
                      pl.BlockSpec((N_HEADS, 1, HEAD_DIM, MOBA_BLOCK), lambda i: (0, i, 0, 0)),
                      pl.BlockSpec((1, 1, ATT_WIDTH), lambda i: (i, 0, 0))]
    else:
        out_shape += [jax.ShapeDtypeStruct((t, ATT_WIDTH), F32)]
        out_specs += [row_spec(ATT_WIDTH)]
    return pl.pallas_call(
        functools.partial(_inproj_kernel, pos0, pos_stride, with_aux),
        grid=grid,
        in_specs=[row_spec(D_MODEL), const((1, D_MODEL)), const((D_MODEL, n_in)), const((1, LANES))],
        out_specs=out_specs, out_shape=out_shape, compiler_params=_cparams(1),
        name="inproj_aux" if with_aux else "inproj",
    )(x, g_mix, w_in_bf, inv_lane)


def _first_max_onehot(g, idx_f, axis):
    mx = jnp.max(g, axis=axis, keepdims=True)
    first = jnp.min(jnp.where(g == mx, idx_f, jnp.inf), axis=axis, keepdims=True)
    return mx, idx_f == first


def _moba_kernel(chunk, q_ref, kp_ref, vt_ref, kmean_ref, o_ref):
    i = pl.program_id(1)
    blk = MOBA_BLOCK
    q2 = q_ref[...]
    km = kmean_ref[...].astype(BF16)
    lane = lax.broadcasted_iota(jnp.int32, (blk, LANES), 1)
    lane_f = lane.astype(F32)
    key_row = lax.broadcasted_iota(jnp.int32, (blk, blk), 0)
    qry_col = lax.broadcasted_iota(jnp.int32, (blk, blk), 1)

    qzs, biases = [], []
    for hh in range(2):
        qz = jnp.where(lane // HEAD_DIM == hh, q2, jnp.zeros_like(q2))
        gate = _dot_nt(qz, km)
        g = jnp.where(lane < i, gate, -jnp.inf)
        sel = jnp.zeros((blk, LANES), jnp.bool_)
        for _ in range(MOBA_TOPK):
            mx, hit = _first_max_onehot(g, lane_f, 1)
            sel = sel | (hit & (mx > -jnp.inf))
            g = jnp.where(hit, -jnp.inf, g)
        qzs.append(qz)
        biases.append(jnp.where(sel, 0.0, MASKED))

    k_own = kp_ref[pl.ds(pl.multiple_of(i * blk, blk), blk), :]
    state = []
    for hh in range(2):
        s = _dot_nt(k_own, qzs[hh])
        s = jnp.where(key_row <= qry_col, s, MASKED)
        m = jnp.max(s, axis=0, keepdims=True)
        p = jnp.exp2(s - m)
        l = jnp.sum(p, axis=0, keepdims=True)
        acc = jnp.dot(vt_ref[hh, i], p.astype(BF16), preferred_element_type=F32)
        state += [m, l, acc]

    def body(c, carry):
        base = c * chunk
        shift = lax.rem(LANES - base, LANES)
        k_augs = []
        for r in range(chunk):
            kr = kp_ref[pl.ds(pl.multiple_of((base + r) * blk, blk), blk), :]
            k_augs.append(jnp.concatenate([kr, (lane == r).astype(BF16)], axis=1))
        out = []
        for hh in range(2):
            m, l, acc = carry[3 * hh:3 * hh + 3]
            bias_c = pltpu.roll(biases[hh], shift, 1).astype(BF16)
            q_aug = jnp.concatenate([qzs[hh], bias_c], axis=1)
            ss = [_dot_nt(ka, q_aug) for ka in k_augs]
            mc = functools.reduce(jnp.maximum, [jnp.max(s, axis=0, keepdims=True) for s in ss])
            m_new = jnp.maximum(m, mc)
            alpha = jnp.exp2(m - m_new)
            l = alpha * l
            acc = alpha * acc
            for r in range(chunk):
                p = jnp.exp2(ss[r] - m_new)
                l = l + jnp.sum(p, axis=0, keepdims=True)
                acc = acc + jnp.dot(vt_ref[hh, base + r], p.astype(BF16), preferred_element_type=F32)
            out += [m_new, l, acc]
        return tuple(out)

    state = lax.fori_loop(0, (i + chunk - 1) // chunk, body, tuple(state))
    o_t = jnp.concatenate([state[2] / state[1], state[5] / state[4]], axis=0)
    o_ref[...] = o_t.T.astype(o_ref.dtype)


def _moba_prompt(qp, kp, vt, kmean):
    t = qp.shape[0]
    nb = t // MOBA_BLOCK
    assert t % MOBA_BLOCK == 0 and nb <= LANES
    chunk = next(c for c in (8, 4, 2, 1) if nb % c == 0)
    km = jnp.pad(kmean.reshape(nb, ATT_WIDTH), ((0, LANES - nb), (0, 0)))
    return pl.pallas_call(
        functools.partial(_moba_kernel, chunk),
        grid=(N_HEADS // 2, nb),
        in_specs=[pl.BlockSpec((MOBA_BLOCK, LANES), lambda hp, i: (i, hp)),
                  pl.BlockSpec((t, LANES), lambda hp, i: (0, hp)),
                  pl.BlockSpec((2, nb, HEAD_DIM, MOBA_BLOCK), lambda hp, i: (hp, 0, 0, 0)),
                  pl.BlockSpec((LANES, LANES), lambda hp, i: (0, hp))],
        out_specs=pl.BlockSpec((MOBA_BLOCK, LANES), lambda hp, i: (i, hp)),
        out_shape=jax.ShapeDtypeStruct((t, ATT_WIDTH), BF16),
        compiler_params=_cparams(2), name="moba",
    )(qp, kp, vt, km)


def _moba_dec_kernel(bps, pt_ref, q_ref, qb_ref, kn_ref, vn_ref, *refs):
    del pt_ref
    k_refs, v_refs = refs[:2 * bps], refs[2 * bps:4 * bps]
    o_ref, gate_s, m_s, l_s, o_s = refs[4 * bps:]
    step = pl.program_id(1)
    blk_lane = lax.broadcasted_iota(jnp.int32, (N_HEADS, LANES), 1)

    @pl.when(step == 0)
    def _():
        gate_s[...] = jnp.full_like(gate_s, -jnp.inf)
        m_s[...] = jnp.zeros_like(m_s)
        l_s[...] = jnp.zeros_like(l_s)

    qb = qb_ref[0]
    for bi in range(bps):
        n = step * bps + bi
        s = jnp.concatenate([jnp.sum(k_refs[2 * bi + w][0] * qb, axis=1) for w in range(2)], axis=1)
        m = jnp.max(s, axis=1, keepdims=True)
        p = jnp.exp(s - m)
        o = jnp.zeros((N_HEADS, HEAD_DIM), F32)
        for w in range(2):
            pw = p[:, w * PAGE_SIZE:(w + 1) * PAGE_SIZE]
            o = o + jnp.sum(v_refs[2 * bi + w][0] * pw[:, None, :], axis=2)
        o_s[n] = o
        here = blk_lane == n
        gate_s[...] = jnp.where(here, jnp.mean(s, axis=1, keepdims=True), gate_s[...])
        m_s[...] = jnp.where(here, m, m_s[...])
        l_s[...] = jnp.where(here, jnp.sum(p, axis=1, keepdims=True), l_s[...])

    @pl.when(step == pl.num_programs(1) - 1)
    def _():
        g = gate_s[...]
        sel = jnp.zeros((N_HEADS, LANES), jnp.bool_)
        for _ in range(MOBA_TOPK):
            mx, hit = _first_max_onehot(g, blk_lane.astype(F32), 1)
            sel = sel | (hit & (mx > -jnp.inf))
            g = jnp.where(hit, -jnp.inf, g)
        s_own = jnp.sum(kn_ref[0] * q_ref[0], axis=1, keepdims=True)
        mm = m_s[...]
        big = jnp.maximum(jnp.max(jnp.where(sel, mm, MASKED), axis=1, keepdims=True), s_own)
        w = jnp.where(sel, jnp.exp(mm - big), 0.0)
        w_own = jnp.exp(s_own - big)
        den = jnp.sum(w * l_s[...], axis=1, keepdims=True) + w_own
        num = w_own * vn_ref[0]
        for nn in range(o_s.shape[0]):
            num = num + w[:, nn:nn + 1] * o_s[nn]
        o_ref[0] = num / den


def _moba_decode(q_s, k_new, v_new, cache_k, cache_v, page_table):
    b = q_s.shape[0]
    n_pages = page_table.shape[1]
    pages_per_blk = MOBA_BLOCK // PAGE_SIZE
    assert pages_per_blk == 2 and n_pages % pages_per_blk == 0
    nbp = n_pages // pages_per_blk
    assert nbp <= LANES
    bps = next(c for c in (4, 2, 1) if nbp % c == 0)
    ck = jnp.transpose(cache_k, (0, 2, 3, 1))
    cv = jnp.transpose(cache_v, (0, 2, 3, 1))
    heads = lambda a: a.reshape(b, N_HEADS, HEAD_DIM)
    q_lanes = jnp.broadcast_to(heads(q_s)[..., None], (b, N_HEADS, HEAD_DIM, LANES))
    row_spec = pl.BlockSpec((1, N_HEADS, HEAD_DIM), lambda s, n, pt: (s, 0, 0))
    page = lambda j: pl.BlockSpec((1, N_HEADS, HEAD_DIM, PAGE_SIZE),
                                  lambda s, n, pt: (pt[s, pages_per_blk * bps * n + j], 0, 0, 0))
    pages = [page(j) for j in range(pages_per_blk * bps)]
    out = pl.pallas_call(
        functools.partial(_moba_dec_kernel, bps),
        grid_spec=pltpu.PrefetchScalarGridSpec(
            num_scalar_prefetch=1, grid=(b, nbp // bps),
            in_specs=[row_spec, pl.BlockSpec((1, N_HEADS, HEAD_DIM, LANES), lambda s, n, pt: (s, 0, 0, 0)),
                      row_spec, row_spec] + pages + pages,
            out_specs=row_spec,
            scratch_shapes=[pltpu.VMEM((N_HEADS, LANES), F32), pltpu.VMEM((N_HEADS, LANES), F32),
                            pltpu.VMEM((N_HEADS, LANES), F32), pltpu.VMEM((nbp, N_HEADS, HEAD_DIM), F32)]),
        out_shape=jax.ShapeDtypeStruct((b, N_HEADS, HEAD_DIM), F32),
        compiler_params=_cparams(2), name="moba_dec",
    )(page_table, heads(q_s), q_lanes, heads(k_new), heads(v_new), *([ck] * len(pages)), *([cv] * len(pages)))
    return out.reshape(b, ATT_WIDTH)


def _post_conv(y, attn_bf, x, b_dw, ln_g, ln_b, w_out):
    y = y + b_dw
    mu = jnp.mean(y, axis=-1, keepdims=True)
    yc = y - mu
    var = jnp.mean(yc * yc, axis=-1, keepdims=True)
    y = yc * lax.rsqrt(var + NORM_EPS) * ln_g + ln_b
    y = y * _sigmoid(y)
    cat = jnp.concatenate([attn_bf, y.astype(BF16)], axis=1)
    return x + jnp.dot(cat, w_out, preferred_element_type=F32)


def _mix_kernel(glu_ref, halo_ref, attn_ref, x_ref, wdw_ref, bdw_ref, lng_ref, lnb_ref, wout_ref, h_ref, xin):
    tm = glu_ref.shape[0]
    first = pl.program_id(0) == 0
    xin[0:CONV_HALO, :] = jnp.where(first, 0.0, halo_ref[...])
    xin[CONV_HALO:, :] = glu_ref[...]
    off = CONV_HALO - (CONV_K - 1)
    y = jnp.zeros((tm, CONV_CH), F32)
    for k in range(CONV_K):
        y = y + wdw_ref[k:k + 1, :] * xin[off + k:off + k + tm, :]
    h_ref[...] = _post_conv(y, attn_ref[...], x_ref[...], bdw_ref[...], lng_ref[...], lnb_ref[...], wout_ref[...])


def _mix_prompt(glu, attn_bf, x, w_dw, b_dw, ln_g, ln_b, w_out_bf, *, tm):
    t = glu.shape[0]
    assert tm % CONV_HALO == 0 and t % tm == 0
    ratio = tm // CONV_HALO
    const = lambda shape: pl.BlockSpec(shape, lambda i: (0,) * len(shape))
    return pl.pallas_call(
        _mix_kernel,
        grid=(t // tm,),
        in_specs=[pl.BlockSpec((tm, CONV_CH), lambda i: (i, 0)),
                  pl.BlockSpec((CONV_HALO, CONV_CH), lambda i: (jnp.maximum(i * ratio - 1, 0), 0)),
                  pl.BlockSpec((tm, ATT_WIDTH), lambda i: (i, 0)),
                  pl.BlockSpec((tm, D_MODEL), lambda i: (i, 0)),
                  const((CONV_HALO, CONV_CH)), const((1, CONV_CH)), const((1, CONV_CH)), const((1, CONV_CH)),
                  const((D_MODEL, D_MODEL))],
        out_specs=pl.BlockSpec((tm, D_MODEL), lambda i: (i, 0)),
        out_shape=jax.ShapeDtypeStruct((t, D_MODEL), F32),
        scratch_shapes=[pltpu.VMEM((tm + CONV_HALO, CONV_CH), F32)],
        compiler_params=_cparams(1), name="mix",
    )(glu, glu, attn_bf, x, w_dw, b_dw, ln_g, ln_b, w_out_bf)


def _mix_dec_kernel(xin_ref, attn_ref, x_ref, wdw_ref, bdw_ref, lng_ref, lnb_ref, wout_ref, h_ref):
    y = jnp.sum(xin_ref[...] * wdw_ref[...][None], axis=1)
    h_ref[...] = _post_conv(y, attn_ref[...].astype(BF16), x_ref[...], bdw_ref[...], lng_ref[...],
                            lnb_ref[...], wout_ref[...])


def _mix_decode(xin, attn, x, w_dw, b_dw, ln_g, ln_b, w_out_bf):
    b = x.shape[0]
    full = lambda a: pl.BlockSpec(a.shape, lambda i: (0,) * a.ndim)
    args = (xin, attn, x, w_dw, b_dw, ln_g, ln_b, w_out_bf)
    return pl.pallas_call(
        _mix_dec_kernel, grid=(1,), in_specs=[full(a) for a in args],
        out_specs=pl.BlockSpec((b, D_MODEL), lambda i: (0, 0)),
        out_shape=jax.ShapeDtypeStruct((b, D_MODEL), F32),
        compiler_params=_cparams(1), name="mix_dec",
    )(*args)


def _hyperbola_cells():
    return [(i, j) for i in range(PEER_TOPK) for j in range(PEER_TOPK) if (i + 1) * (j + 1) <= PEER_TOPK]


def _route_kernel(h_ref, g_ref, wq_ref, keys_ref, xn_ref, nc1_ref, r2_ref, e1_ref, e2_ref,
                  sc_s, rank_s, top_s, cnt_s, z_s):
    tm = h_ref.shape[0]
    nk = PEER_NKEYS
    xn = _rmsnorm(h_ref[...], g_ref[...]).astype(BF16)
    xn_ref[...] = xn
    qb = jnp.dot(xn, wq_ref[...], preferred_element_type=F32).astype(BF16)
    for hc in range(2 * PEER_HEADS):
        sc_s[hc] = _dot_nt(keys_ref[hc], qb[:, hc * nk:(hc + 1) * nk])

    key_f = lax.broadcasted_iota(jnp.int32, (nk, tm), 0).astype(F32)
    sub = lax.broadcasted_iota(jnp.int32, (PEER_HEADS, tm), 0)
    cells = _hyperbola_cells()

    def rank_rounds(hcs, exact_ties):
        xs = [sc_s[hc] for hc in hcs]
        ranks = [jnp.full((nk, tm), float(PEER_TOPK), F32) for _ in hcs]
        for r in range(PEER_TOPK):
            for n, hc in enumerate(hcs):
                if exact_ties:
                    mx, hit = _first_max_onehot(xs[n], key_f, 0)
                else:
                    mx = jnp.max(xs[n], axis=0, keepdims=True)
                    hit = xs[n] == mx
                ranks[n] = jnp.where(hit, float(r), ranks[n])
                xs[n] = jnp.where(hit, -jnp.inf, xs[n])
                top_s[hc, r] = jnp.broadcast_to(mx, (8, tm))
        for n, hc in enumerate(hcs):
            rank_s[hc] = ranks[n]
        return ranks

    def rank_body(h, carry):
        hcs = (2 * h, 2 * h + 1)
        ranks = rank_rounds(hcs, exact_ties=False)
        ranked = sum(jnp.sum(jnp.where(rk < float(PEER_TOPK), 1.0, 0.0), axis=0, keepdims=True) for rk in ranks)
        tied = jnp.max(jnp.abs(ranked - float(len(hcs) * PEER_TOPK))) > 0.0

        @pl.when(tied)
        def _():
            rank_rounds(hcs, exact_ties=True)

        return carry

    lax.fori_loop(0, PEER_HEADS, rank_body, 0)

    def heads_on_sublanes(side, r):
        out = top_s[side, r]
        for h in range(1, PEER_HEADS):
            out = jnp.where(sub == h, top_s[2 * h + side, r], out)
        return out

    a = [heads_on_sublanes(0, r) for r in range(PEER_TOPK)]
    bb = [heads_on_sublanes(1, r) for r in range(PEER_TOPK)]
    cur0 = tuple(a[i] + bb[j] for (i, j) in cells)
    best = cur0[0]

    def pick_body(_, carry):
        cur, taken = carry
        mx = functools.reduce(jnp.maximum, cur)
        first = functools.reduce(jnp.minimum, [jnp.where(cur[ci] == mx, float(ci), jnp.inf)
                                               for ci in range(len(cells))])
        hits = [first == float(ci) for ci in range(len(cells))]
        return (tuple(jnp.where(hit, -jnp.inf, c) for hit, c in zip(hits, cur)),
                tuple(jnp.where(hit, 1.0, t) for hit, t in zip(hits, taken)))

    _, taken = lax.fori_loop(0, PEER_TOPK, pick_body, (cur0, tuple(jnp.zeros_like(best) for _ in cells)))
    z = jnp.zeros_like(best)
    counts = [jnp.zeros_like(best) for _ in range(PEER_TOPK)]
    for ci, (i, j) in enumerate(cells):
        counts[i] = counts[i] + taken[ci]
        z = z + taken[ci] * jnp.exp(a[i] + bb[j] - best)
    for h in range(PEER_HEADS):
        z_s[h] = jnp.broadcast_to(z[h:h + 1], (8, tm))
        for r in range(PEER_TOPK):
            cnt_s[h, r] = jnp.broadcast_to(counts[r][h:h + 1], (8, tm))

    def gate_body(h, carry):
        r1 = rank_s[2 * h]
        nc = jnp.zeros((nk, tm), F32)
        for r in range(PEER_TOPK):
            nc = jnp.where(r1 == float(r), cnt_s[h, r, 0:1, :], nc)
        nc1_ref[h] = nc
        r2_ref[h] = rank_s[2 * h + 1].astype(r2_ref.dtype)
        a0 = top_s[2 * h, 0, 0:1, :]
        b0 = top_s[2 * h + 1, 0, 0:1, :]
        e1_ref[h] = jnp.exp(sc_s[2 * h] - a0) / z_s[h, 0:1, :]
        e2_ref[h] = jnp.exp(sc_s[2 * h + 1] - b0).astype(e2_ref.dtype)
        return carry

    lax.fori_loop(0, PEER_HEADS, gate_body, 0)


def _peer_route(h1, g_ffn, wq_bf, keys_bf, *, tm):
    t = h1.shape[0]
    assert t % tm == 0 and tm % LANES == 0
    const = lambda shape: pl.BlockSpec(shape, lambda i: (0,) * len(shape))
    gate_shape = lambda dt: jax.ShapeDtypeStruct((PEER_HEADS, PEER_NKEYS, t), dt)
    gate_spec = pl.BlockSpec((PEER_HEADS, PEER_NKEYS, tm), lambda i: (0, 0, i))
    return pl.pallas_call(
        _route_kernel,
        grid=(t // tm,),
        in_specs=[pl.BlockSpec((tm, D_MODEL), lambda i: (i, 0)), const((1, D_MODEL)),
                  const(wq_bf.shape), const(keys_bf.shape)],
        out_specs=[pl.BlockSpec((tm, D_MODEL), lambda i: (i, 0))] + [gate_spec] * 4,
        out_shape=[jax.ShapeDtypeStruct((t, D_MODEL), BF16),
                   gate_shape(F32), gate_shape(BF16), gate_shape(F32), gate_shape(BF16)],
        scratch_shapes=[pltpu.VMEM((2 * PEER_HEADS, PEER_NKEYS, tm), F32),
                        pltpu.VMEM((2 * PEER_HEADS, PEER_NKEYS, tm), F32),
                        pltpu.VMEM((2 * PEER_HEADS, PEER_TOPK, 8, tm), F32),
                        pltpu.VMEM((PEER_HEADS, PEER_TOPK, 8, tm), F32),
                        pltpu.VMEM((PEER_HEADS, 8, tm), F32)],
        compiler_params=_cparams(1), name="route",
    )(h1, g_ffn, wq_bf, keys_bf)


def _experts_kernel(xn_ref, u_ref, vt_ref, nc1_ref, r2_ref, e1_ref, e2_ref, h_ref, o_ref, acc, w_s):
    c = pl.program_id(1)
    last = pl.num_programs(1) - 1

    @pl.when(c == 0)
    def _():
        acc[...] = jnp.zeros_like(acc)
        w_s[...] = jnp.zeros_like(w_s)

    acc[...] += jnp.dot(vt_ref[...], w_s[...], preferred_element_type=F32)

    blk = jnp.minimum(c, last - 1)
    pre = _dot_nt(u_ref[...], xn_ref[...])
    act = (0.5 * pre * (1.0 + lax.erf(pre * (2.0 ** -0.5)))).astype(BF16)
    n_i1 = EXPERT_BLOCK // PEER_NKEYS
    zero = jnp.zeros((PEER_NKEYS, act.shape[1]), BF16)
    gates = [zero] * n_i1
    for h in range(PEER_HEADS):
        r2 = r2_ref[h]
        e2 = e2_ref[h]
        for u in range(n_i1):
            i1 = blk * n_i1 + u
            nc = nc1_ref[h, pl.ds(i1, 1), :].astype(BF16)
            e1 = e1_ref[h, pl.ds(i1, 1), :].astype(BF16)
            gates[u] = gates[u] + jnp.where(r2 < nc, e2, zero) * e1
    w_s[...] = jnp.concatenate(gates, axis=0) * act

    @pl.when(c == last)
    def _():
        o_ref[...] = h_ref[...] + acc[...].T


def _peer_experts(xn, u_bf, vt_bf, nc1, r2, e1, e2, h1, *, tm):
    t = xn.shape[0]
    n_exp = u_bf.shape[0]
    assert t % tm == 0 and n_exp % EXPERT_BLOCK == 0
    n_blk = n_exp // EXPERT_BLOCK
    once = dict(pipeline_mode=pl.Buffered(1))
    gate_spec = pl.BlockSpec((PEER_HEADS, PEER_NKEYS, tm), lambda i, c: (0, 0, i), **once)
    tok_spec = pl.BlockSpec((tm, D_MODEL), lambda i, c: (i, 0))
    return pl.pallas_call(
        _experts_kernel,
        grid=(t // tm, n_blk + 1),
        in_specs=[tok_spec,
                  pl.BlockSpec((EXPERT_BLOCK, D_MODEL), lambda i, c: (jnp.minimum(c, n_blk - 1), 0)),
                  pl.BlockSpec((D_MODEL, EXPERT_BLOCK), lambda i, c: (0, jnp.maximum(c - 1, 0))),
                  gate_spec, gate_spec, gate_spec, gate_spec,
                  pl.BlockSpec((tm, D_MODEL), lambda i, c: (i, 0), **once)],
        out_specs=tok_spec,
        out_shape=jax.ShapeDtypeStruct((t, D_MODEL), F32),
        scratch_shapes=[pltpu.VMEM((D_MODEL, tm), F32), pltpu.VMEM((EXPERT_BLOCK, tm), BF16)],
        compiler_params=_cparams(2), name="experts",
    )(xn, u_bf, vt_bf, nc1, r2, e1, e2, h1)


def _ple_kernel(h_ref, p_ref, gple_ref, wg_ref, wp_ref, gfin_ref, y_ref):
    h = h_ref[...]
    gate = _sigmoid(jnp.dot(_rmsnorm(h, gple_ref[...]).astype(BF16), wg_ref[...], preferred_element_type=F32))
    proj = jnp.dot(p_ref[...].astype(BF16), wp_ref[...], preferred_element_type=F32)
    y_ref[...] = _rmsnorm(h + gate * proj, gfin_ref[...])


def _ple_final(h2, p_emb, g_ple, wg_bf, wp_bf, g_final, *, tm):
    t = h2.shape[0]
    const = lambda shape: pl.BlockSpec(shape, lambda i: (0,) * len(shape))
    return pl.pallas_call(
        _ple_kernel,
        grid=(t // tm,),
        in_specs=[pl.BlockSpec((tm, D_MODEL), lambda i: (i, 0)), pl.BlockSpec((tm, PLE_DIM), lambda i: (i, 0)),
                  const((1, D_MODEL)), const((D_MODEL, D_MODEL)), const((PLE_DIM, D_MODEL)), const((1, D_MODEL))],
        out_specs=pl.BlockSpec((tm, D_MODEL), lambda i: (i, 0)),
        out_shape=jax.ShapeDtypeStruct((t, D_MODEL), F32),
        compiler_params=_cparams(1), name="ple",
    )(h2, p_emb, g_ple, wg_bf, wp_bf, g_final)


def _pick_tile(t, want):
    tm = min(want, t)
    assert t % tm == 0
    return tm


def _ffn(h1, p_emb, lw, g_final):
    t = h1.shape[0]
    xn, nc1, r2, e1, e2 = _peer_route(h1, lw["g_ffn"], lw["wq"], lw["keys"], tm=_pick_tile(t, 256))
    h2 = _peer_experts(xn, lw["u"], lw["vt"], nc1, r2, e1, e2, h1, tm=_pick_tile(t, 1024))
    return _ple_final(h2, p_emb, lw["g_ple"], lw["w_ple_gate"], lw["w_ple_proj"], g_final, tm=_pick_tile(t, 256))


def kernel(x_prompt, x_sample, cache_k, cache_v, cache_conv, page_table, p_prompt, p_sample,
           g_mix, w_in, w_dw, b_dw, ln_g, ln_b, w_out, g_ffn, peer_wq, peer_keys, peer_u, peer_v,
           g_ple, w_ple_gate, w_ple_proj, g_final):
    depth = g_mix.shape[0]
    batch, seq, _ = x_prompt.shape
    dec_batch, dec_seq, _ = x_sample.shape
    assert depth == 1 and batch == 1 and dec_seq == 1
    past_len = page_table.shape[1] * PAGE_SIZE
    assert seq % MOBA_BLOCK == 0 and past_len % MOBA_BLOCK == 0 and seq >= CONV_K - 1
    row = lambda a: a.reshape(1, -1)
    lane = jnp.arange(LANES) % (ROT_DIM // 2)
    inv_lane = (ROPE_THETA ** (-lane.astype(F32) * 2.0 / ROT_DIM)).reshape(1, LANES)
    l = 0
    lw = dict(
        g_ffn=row(g_ffn[l]), wq=peer_wq[l].astype(BF16),
        keys=peer_keys[l].reshape(2 * PEER_HEADS, PEER_NKEYS, -1).astype(BF16),
        u=peer_u[l].astype(BF16), vt=peer_v[l].T.astype(BF16),
        g_ple=row(g_ple[l]), w_ple_gate=w_ple_gate[l].astype(BF16), w_ple_proj=w_ple_proj[l].astype(BF16))
    w_in_bf = w_in[l].astype(BF16)
    w_out_bf = w_out[l].astype(BF16)
    w_dw_pad = jnp.pad(w_dw[l], ((0, CONV_HALO - CONV_K), (0, 0)))
    conv_args = (w_dw_pad, row(b_dw[l]), row(ln_g[l]), row(ln_b[l]), w_out_bf)
    g_fin = row(g_final)

    xp = x_prompt.reshape(seq, D_MODEL)
    k_p, v_p, glu_p, qp, kp, vt, kmean = _inproj(xp, row(g_mix[l]), w_in_bf, inv_lane, pos0=0, pos_stride=1,
                                                 tm=MOBA_BLOCK, with_aux=True)
    attn_p = _moba_prompt(qp, kp, vt, kmean)
    h1_p = _mix_prompt(glu_p, attn_p, xp, *conv_args, tm=_pick_tile(seq, 256))
    y_p = _ffn(h1_p, p_prompt[l].reshape(seq, PLE_DIM), lw, g_fin)

    xs = x_sample.reshape(dec_batch, D_MODEL)
    k_s, v_s, glu_s, q_s = _inproj(xs, row(g_mix[l]), w_in_bf, inv_lane, pos0=past_len, pos_stride=0,
                                   tm=dec_batch, with_aux=False)
    attn_s = _moba_decode(q_s, k_s, v_s, cache_k[l], cache_v[l], page_table)
    xin_s = jnp.concatenate([cache_conv[l], glu_s[:, None, :]], axis=1)
    xin_pad = jnp.pad(xin_s, ((0, 0), (0, CONV_HALO - CONV_K), (0, 0)))
    h1_s = _mix_decode(xin_pad, attn_s, xs, *conv_args)
    pad = (-dec_batch) % LANES
    h1_s = jnp.pad(h1_s, ((0, pad), (0, 0)))
    p_s = jnp.pad(p_sample[l].reshape(dec_batch, PLE_DIM), ((0, pad), (0, 0)))
    y_s = _ffn(h1_s, p_s, lw, g_fin)[:dec_batch]

    conv_p = glu_p[seq - (CONV_K - 1):].reshape(1, 1, CONV_K - 1, CONV_CH)
    conv_s = xin_s[:, 1:].reshape(1, dec_batch, CONV_K - 1, CONV_CH)
    return (y_p.reshape(1, seq, D_MODEL), y_s.reshape(dec_batch, 1, D_MODEL),
            k_p.reshape(1, 1, seq, N_HEADS, HEAD_DIM), v_p.reshape(1, 1, seq, N_HEADS, HEAD_DIM), conv_p,
            k_s.reshape(1, dec_batch, 1, N_HEADS, HEAD_DIM), v_s.reshape(1, dec_batch, 1, N_HEADS, HEAD_DIM),
            conv_s)
```

```python
import functools

import jax
import jax.numpy as jnp
from jax import lax
from jax.experimental import pallas as pl
from jax.experimental.pallas import tpu as pltpu

F32 = jnp.float32
BF16 = jnp.bfloat16

D_MODEL = 1024
N_HEADS = 8
HEAD_DIM = 64
ATT_WIDTH = N_HEADS * HEAD_DIM
CONV_CH = D_MODEL - ATT_WIDTH
ROT_DIM = HEAD_DIM // 4
ROPE_THETA = 500000.0
MOBA_BLOCK = 256
MOBA_TOPK = 3
CONV_K = 31
PAGE_SIZE = 128
PEER_HEADS = 8
PEER_NKEYS = 128
PEER_TOPK = 16
PLE_DIM = 256
NORM_EPS = 1e-6

LANES = 128
CONV_HALO = 32
MASKED = -1e30
EXPERT_BLOCK = 2 * PEER_NKEYS
VMEM_LIMIT = 56 * 1024 * 1024
LOG2_E = 1.4426950408889634


def _cparams(n_axes):
    return pltpu.CompilerParams(dimension_semantics=("arbitrary",) * n_axes, vmem_limit_bytes=VMEM_LIMIT)


def _rmsnorm(x, g):
    return x * lax.rsqrt(jnp.mean(x * x, axis=-1, keepdims=True) + NORM_EPS) * g


def _sigmoid(x):
    return 1.0 / (1.0 + jnp.exp(-x))


def _dot_nt(a, b):
    return lax.dot_general(a, b, (((1,), (1,)), ((), ())), preferred_element_type=F32)


def _rope(x, cos4, sin4, low4):
    half = ROT_DIM // 2
    from_hi = pltpu.roll(x, ATT_WIDTH - half, 1)
    from_lo = pltpu.roll(x, half, 1)
    return x * cos4 + jnp.where(low4, from_hi, from_lo) * sin4


def _inproj_kernel(pos0, pos_stride, with_aux, x_ref, g_ref, w_ref, inv_ref, *out_refs):
    if with_aux:
        k_ref, v_ref, glu_ref, q_ref, kp_ref, vt_ref, kmean_ref = out_refs
    else:
        k_ref, v_ref, glu_ref, q_ref = out_refs
    tm = x_ref.shape[0]
    xn = _rmsnorm(x_ref[...], g_ref[...]).astype(BF16)
    z = jnp.dot(xn, w_ref[...], preferred_element_type=F32)
    q = z[:, 0 * ATT_WIDTH:1 * ATT_WIDTH]
    k = z[:, 1 * ATT_WIDTH:2 * ATT_WIDTH]
    v = z[:, 2 * ATT_WIDTH:3 * ATT_WIDTH]
    ga = z[:, 3 * ATT_WIDTH:3 * ATT_WIDTH + CONV_CH]
    gb = z[:, 3 * ATT_WIDTH + CONV_CH:]

    row = lax.broadcasted_iota(jnp.int32, (tm, LANES), 0)
    d = lax.broadcasted_iota(jnp.int32, (tm, LANES), 1) % HEAD_DIM
    pos = (pos0 + (pl.program_id(0) * tm + row) * pos_stride).astype(F32)
    ang = pos * inv_ref[...]
    cos = jnp.where(d < ROT_DIM, jnp.cos(ang), 1.0)
    sin = jnp.sin(ang)
    sin = jnp.where(d < ROT_DIM // 2, -sin, jnp.where(d < ROT_DIM, sin, 0.0))
    reps = ATT_WIDTH // LANES
    cos4 = jnp.concatenate([cos] * reps, axis=1)
    sin4 = jnp.concatenate([sin] * reps, axis=1)
    low4 = lax.broadcasted_iota(jnp.int32, (tm, ATT_WIDTH), 1) % HEAD_DIM < ROT_DIM // 2

    k = _rope(k, cos4, sin4, low4)
    q = _rope(q, cos4, sin4, low4) * (HEAD_DIM ** -0.5)
    k_ref[...] = k
    v_ref[...] = v
    glu_ref[...] = ga * _sigmoid(gb)
    if with_aux:
        q_ref[...] = (q * LOG2_E).astype(BF16)
        kp_ref[...] = k.astype(BF16)
        kmean_ref[0] = jnp.mean(k, axis=0, keepdims=True)
        vt = v.T
        for h in range(N_HEADS):
            vt_ref[h, 0] = vt[h * HEAD_DIM:(h + 1) * HEAD_DIM, :].astype(BF16)
    else:
        q_ref[...] = q


def _inproj(x, g_mix, w_in_bf, inv_lane, *, pos0, pos_stride, tm, with_aux):
    t = x.shape[0]
    n_in = w_in_bf.shape[1]
    grid = (t // tm,)
    row_spec = lambda w: pl.BlockSpec((tm, w), lambda i: (i, 0))
    const = lambda shape: pl.BlockSpec(shape, lambda i: (0,) * len(shape))
    out_shape = [jax.ShapeDtypeStruct((t, ATT_WIDTH), F32), jax.ShapeDtypeStruct((t, ATT_WIDTH), F32),
                 jax.ShapeDtypeStruct((t, CONV_CH), F32)]
    out_specs = [row_spec(ATT_WIDTH), row_spec(ATT_WIDTH), row_spec(CONV_CH)]
    if with_aux:
        assert tm == MOBA_BLOCK
        nb = t // MOBA_BLOCK
        out_shape += [jax.ShapeDtypeStruct((t, ATT_WIDTH), BF16), jax.ShapeDtypeStruct((t, ATT_WIDTH), BF16),
                      jax.ShapeDtypeStruct((N_HEADS, nb, HEAD_DIM, MOBA_BLOCK), BF16),
                      jax.ShapeDtypeStruct((nb, 1, ATT_WIDTH), F32)]
        out_specs += [row_spec(ATT_WIDTH), row_spec(ATT_WIDTH),
                      pl.BlockSpec((N_HEADS, 1, HEAD_DIM, MOBA_BLOCK), lambda i: (0, i, 0, 0)),
                      pl.BlockSpec((1, 1, ATT_WIDTH), lambda i: (i, 0, 0))]
    else:
        out_shape += [jax.ShapeDtypeStruct((t, ATT_WIDTH), F32)]
        out_specs += [row_spec(ATT_WIDTH)]
    return pl.pallas_call(
        functools.partial(_inproj_kernel, pos0, pos_stride, with_aux),
        grid=grid,
        in_specs=[row_spec(D_MODEL), const((1, D_MODEL)), const((D_MODEL, n_in)), const((1, LANES))],
        out_specs=out_specs, out_shape=out_shape, compiler_params=_cparams(1),
        name="inproj_aux" if with_aux else "inproj",
    )(x, g_mix, w_in_bf, inv_lane)


def _first_max_onehot(g, idx_f, axis):
    mx = jnp.max(g, axis=axis, keepdims=True)
    first = jnp.min(jnp.where(g == mx, idx_f, jnp.inf), axis=axis, keepdims=True)
    return mx, idx_f == first


def _moba_kernel(chunk, q_ref, kp_ref, vt_ref, kmean_ref, o_ref):
    i = pl.program_id(1)
    blk = MOBA_BLOCK
    q2 = q_ref[...]
    km = kmean_ref[...].astype(BF16)
    lane = lax.broadcasted_iota(jnp.int32, (blk, LANES), 1)
    lane_f = lane.astype(F32)
    key_row = lax.broadcasted_iota(jnp.int32, (blk, blk), 0)
    qry_col = lax.broadcasted_iota(jnp.int32, (blk, blk), 1)

    qzs, biases = [], []
    for hh in range(2):
        qz = jnp.where(lane // HEAD_DIM == hh, q2, jnp.zeros_like(q2))
        gate = _dot_nt(qz, km)
        g = jnp.where(lane < i, gate, -jnp.inf)
        sel = jnp.zeros((blk, LANES), jnp.bool_)
        for _ in range(MOBA_TOPK):
            mx, hit = _first_max_onehot(g, lane_f, 1)
            sel = sel | (hit & (mx > -jnp.inf))
            g = jnp.where(hit, -jnp.inf, g)
        qzs.append(qz)
        biases.append(jnp.where(sel, 0.0, MASKED))

    k_own = kp_ref[pl.ds(pl.multiple_of(i * blk, blk), blk), :]
    state = []
    for hh in range(2):
        s = _dot_nt(k_own, qzs[hh])
        s = jnp.where(key_row <= qry_col, s, MASKED)
        m = jnp.max(s, axis=0, keepdims=True)
        p = jnp.exp2(s - m)
        l = jnp.sum(p, axis=0, keepdims=True)
        acc = jnp.dot(vt_ref[hh, i], p.astype(BF16), preferred_element_type=F32)
        state += [m, l, acc]

    def body(c, carry):
        base = c * chunk
        shift = lax.rem(LANES - base, LANES)
        k_augs = []
        for r in range(chunk):
            kr = kp_ref[pl.ds(pl.multiple_of((base + r) * blk, blk), blk), :]
            k_augs.append(jnp.concatenate([kr, (lane == r).astype(BF16)], axis=1))
        out = []
        for hh in range(2):
            m, l, acc = carry[3 * hh:3 * hh + 3]
            bias_c = pltpu.roll(biases[hh], shift, 1).astype(BF16)
            q_aug = jnp.concatenate([qzs[hh], bias_c], axis=1)
            ss = [_dot_nt(ka, q_aug) for ka in k_augs]
            mc = functools.reduce(jnp.maximum, [jnp.max(s, axis=0, keepdims=True) for s in ss])
            m_new = jnp.maximum(m, mc)
            alpha = jnp.exp2(m - m_new)
            l = alpha * l
            acc = alpha * acc
            for r in range(chunk):
                p = jnp.exp2(ss[r] - m_new)
                l = l + jnp.sum(p, axis=0, keepdims=True)
                acc = acc + jnp.dot(vt_ref[hh, base + r], p.astype(BF16), preferred_element_type=F32)
            out += [m_new, l, acc]
        return tuple(out)

    state = lax.fori_loop(0, (i + chunk - 1) // chunk, body, tuple(state))
    o_t = jnp.concatenate([state[2] / state[1], state[5] / state[4]], axis=0)
    o_ref[...] = o_t.T.astype(o_ref.dtype)


def _moba_prompt(qp, kp, vt, kmean):
    t = qp.shape[0]
    nb = t // MOBA_BLOCK
    assert t % MOBA_BLOCK == 0 and nb <= LANES
    chunk = next(c for c in (8, 4, 2, 1) if nb % c == 0)
    km = jnp.pad(kmean.reshape(nb, ATT_WIDTH), ((0, LANES - nb), (0, 0)))
    return pl.pallas_call(
        functools.partial(_moba_kernel, chunk),
        grid=(N_HEADS // 2, nb),
        in_specs=[pl.BlockSpec((MOBA_BLOCK, LANES), lambda hp, i: (i, hp)),
                  pl.BlockSpec((t, LANES), lambda hp, i: (0, hp)),
                  pl.BlockSpec((2, nb, HEAD_DIM, MOBA_BLOCK), lambda hp, i: (hp, 0, 0, 0)),
                  pl.BlockSpec((LANES, LANES), lambda hp, i: (0, hp))],
        out_specs=pl.BlockSpec((MOBA_BLOCK, LANES), lambda hp, i: (i, hp)),
        out_shape=jax.ShapeDtypeStruct((t, ATT_WIDTH), BF16),
        compiler_params=_cparams(2), name="moba",
    )(qp, kp, vt, km)


def _moba_dec_kernel(bps, pt_ref, q_ref, qb_ref, kn_ref, vn_ref, *refs):
    del pt_ref
    k_refs, v_refs = refs[:2 * bps], refs[2 * bps:4 * bps]
    o_ref, gate_s, m_s, l_s, o_s = refs[4 * bps:]
    step = pl.program_id(1)
    blk_lane = lax.broadcasted_iota(jnp.int32, (N_HEADS, LANES), 1)

    @pl.when(step == 0)
    def _():
        gate_s[...] = jnp.full_like(gate_s, -jnp.inf)
        m_s[...] = jnp.zeros_like(m_s)
        l_s[...] = jnp.zeros_like(l_s)

    qb = qb_ref[0]
    for bi in range(bps):
        n = step * bps + bi
        s = jnp.concatenate([jnp.sum(k_refs[2 * bi + w][0] * qb, axis=1) for w in range(2)], axis=1)
        m = jnp.max(s, axis=1, keepdims=True)
        p = jnp.exp(s - m)
        pv = sum(v_refs[2 * bi + w][0] * p[:, None, w * PAGE_SIZE:(w + 1) * PAGE_SIZE] for w in range(2))
        o_s[n] = jnp.sum(pv, axis=2)
        here = blk_lane == n
        gate_s[...] = jnp.where(here, jnp.mean(s, axis=1, keepdims=True), gate_s[...])
        m_s[...] = jnp.where(here, m, m_s[...])
        l_s[...] = jnp.where(here, jnp.sum(p, axis=1, keepdims=True), l_s[...])

    @pl.when(step == pl.num_programs(1) - 1)
    def _():
        g = gate_s[...]
        sel = jnp.zeros((N_HEADS, LANES), jnp.bool_)
        for _ in range(MOBA_TOPK):
            mx, hit = _first_max_onehot(g, blk_lane.astype(F32), 1)
            sel = sel | (hit & (mx > -jnp.inf))
            g = jnp.where(hit, -jnp.inf, g)
        s_own = jnp.sum(kn_ref[0] * q_ref[0], axis=1, keepdims=True)
        mm = m_s[...]
        big = jnp.maximum(jnp.max(jnp.where(sel, mm, MASKED), axis=1, keepdims=True), s_own)
        w = jnp.where(sel, jnp.exp(mm - big), 0.0)
        w_own = jnp.exp(s_own - big)
        den = jnp.sum(w * l_s[...], axis=1, keepdims=True) + w_own
        num = w_own * vn_ref[0]
        for nn in range(o_s.shape[0]):
            num = num + w[:, nn:nn + 1] * o_s[nn]
        o_ref[0] = num / den


def _moba_decode(q_s, k_new, v_new, cache_k, cache_v, page_table):
    b = q_s.shape[0]
    n_pages = page_table.shape[1]
    pages_per_blk = MOBA_BLOCK // PAGE_SIZE
    assert pages_per_blk == 2 and n_pages % pages_per_blk == 0
    nbp = n_pages // pages_per_blk
    assert nbp <= LANES
    bps = next(c for c in (8, 4, 2, 1) if nbp % c == 0)
    ck = jnp.transpose(cache_k, (0, 2, 3, 1))
    cv = jnp.transpose(cache_v, (0, 2, 3, 1))
    heads = lambda a: a.reshape(b, N_HEADS, HEAD_DIM)
    q_lanes = jnp.broadcast_to(heads(q_s)[..., None], (b, N_HEADS, HEAD_DIM, LANES))
    row_spec = pl.BlockSpec((1, N_HEADS, HEAD_DIM), lambda s, n, pt: (s, 0, 0))
    page = lambda j: pl.BlockSpec((1, N_HEADS, HEAD_DIM, PAGE_SIZE),
                                  lambda s, n, pt: (pt[s, pages_per_blk * bps * n + j], 0, 0, 0))
    pages = [page(j) for j in range(pages_per_blk * bps)]
    out = pl.pallas_call(
        functools.partial(_moba_dec_kernel, bps),
        grid_spec=pltpu.PrefetchScalarGridSpec(
            num_scalar_prefetch=1, grid=(b, nbp // bps),
            in_specs=[row_spec, pl.BlockSpec((1, N_HEADS, HEAD_DIM, LANES), lambda s, n, pt: (s, 0, 0, 0)),
                      row_spec, row_spec] + pages + pages,
            out_specs=row_spec,
            scratch_shapes=[pltpu.VMEM((N_HEADS, LANES), F32), pltpu.VMEM((N_HEADS, LANES), F32),
                            pltpu.VMEM((N_HEADS, LANES), F32), pltpu.VMEM((nbp, N_HEADS, HEAD_DIM), F32)]),
        out_shape=jax.ShapeDtypeStruct((b, N_HEADS, HEAD_DIM), F32),
        compiler_params=_cparams(2), name="moba_dec",
    )(page_table, heads(q_s), q_lanes, heads(k_new), heads(v_new), *([ck] * len(pages)), *([cv] * len(pages)))
    return out.reshape(b, ATT_WIDTH)


def _post_conv(y, attn_bf, x, b_dw, ln_g, ln_b, w_out):
    y = y + b_dw
    mu = jnp.mean(y, axis=-1, keepdims=True)
    yc = y - mu
    var = jnp.mean(yc * yc, axis=-1, keepdims=True)
    y = yc * lax.rsqrt(var + NORM_EPS) * ln_g + ln_b
    y = y * _sigmoid(y)
    cat = jnp.concatenate([attn_bf, y.astype(BF16)], axis=1)
    return x + jnp.dot(cat, w_out, preferred_element_type=F32)


def _mix_kernel(glu_ref, halo_ref, attn_ref, x_ref, wdw_ref, bdw_ref, lng_ref, lnb_ref, wout_ref, h_ref, xin):
    tm = glu_ref.shape[0]
    first = pl.program_id(0) == 0
    xin[0:CONV_HALO, :] = jnp.where(first, 0.0, halo_ref[...])
    xin[CONV_HALO:, :] = glu_ref[...]
    off = CONV_HALO - (CONV_K - 1)
    y = jnp.zeros((tm, CONV_CH), F32)
    for k in range(CONV_K):
        y = y + wdw_ref[k:k + 1, :] * xin[off + k:off + k + tm, :]
    h_ref[...] = _post_conv(y, attn_ref[...], x_ref[...], bdw_ref[...], lng_ref[...], lnb_ref[...], wout_ref[...])


def _mix_prompt(glu, attn_bf, x, w_dw, b_dw, ln_g, ln_b, w_out_bf, *, tm):
    t = glu.shape[0]
    assert tm % CONV_HALO == 0 and t % tm == 0
    ratio = tm // CONV_HALO
    const = lambda shape: pl.BlockSpec(shape, lambda i: (0,) * len(shape))
    return pl.pallas_call(
        _mix_kernel,
        grid=(t // tm,),
        in_specs=[pl.BlockSpec((tm, CONV_CH), lambda i: (i, 0)),
                  pl.BlockSpec((CONV_HALO, CONV_CH), lambda i: (jnp.maximum(i * ratio - 1, 0), 0)),
                  pl.BlockSpec((tm, ATT_WIDTH), lambda i: (i, 0)),
                  pl.BlockSpec((tm, D_MODEL), lambda i: (i, 0)),
                  const((CONV_HALO, CONV_CH)), const((1, CONV_CH)), const((1, CONV_CH)), const((1, CONV_CH)),
                  const((D_MODEL, D_MODEL))],
        out_specs=pl.BlockSpec((tm, D_MODEL), lambda i: (i, 0)),
        out_shape=jax.ShapeDtypeStruct((t, D_MODEL), F32),
        scratch_shapes=[pltpu.VMEM((tm + CONV_HALO, CONV_CH), F32)],
        compiler_params=_cparams(1), name="mix",
    )(glu, glu, attn_bf, x, w_dw, b_dw, ln_g, ln_b, w_out_bf)


def _mix_dec_kernel(xin_ref, attn_ref, x_ref, wdw_ref, bdw_ref, lng_ref, lnb_ref, wout_ref, h_ref):
    y = jnp.sum(xin_ref[...] * wdw_ref[...][None], axis=1)
    h_ref[...] = _post_conv(y, attn_ref[...].astype(BF16), x_ref[...], bdw_ref[...], lng_ref[...],
                            lnb_ref[...], wout_ref[...])


def _mix_decode(xin, attn, x, w_dw, b_dw, ln_g, ln_b, w_out_bf):
    b = x.shape[0]
    full = lambda a: pl.BlockSpec(a.shape, lambda i: (0,) * a.ndim)
    args = (xin, attn, x, w_dw, b_dw, ln_g, ln_b, w_out_bf)
    return pl.pallas_call(
        _mix_dec_kernel, grid=(1,), in_specs=[full(a) for a in args],
        out_specs=pl.BlockSpec((b, D_MODEL), lambda i: (0, 0)),
        out_shape=jax.ShapeDtypeStruct((b, D_MODEL), F32),
        compiler_params=_cparams(1), name="mix_dec",
    )(*args)


def _hyperbola_cells():
    return [(i, j) for i in range(PEER_TOPK) for j in range(PEER_TOPK) if (i + 1) * (j + 1) <= PEER_TOPK]


def _route_kernel(h_ref, g_ref, wq_ref, keys_ref, xn_ref, nc1_ref, r2_ref, e1_ref, e2_ref,
                  sc_s, rank_s, top_s, cnt_s, z_s):
    tm = h_ref.shape[0]
    nk = PEER_NKEYS
    xn = _rmsnorm(h_ref[...], g_ref[...]).astype(BF16)
    xn_ref[...] = xn
    qb = jnp.dot(xn, wq_ref[...], preferred_element_type=F32).astype(BF16)
    for hc in range(2 * PEER_HEADS):
        sc_s[hc] = _dot_nt(keys_ref[hc], qb[:, hc * nk:(hc + 1) * nk])

    key_f = lax.broadcasted_iota(jnp.int32, (nk, tm), 0).astype(F32)
    sub = lax.broadcasted_iota(jnp.int32, (PEER_HEADS, tm), 0)
    cells = _hyperbola_cells()

    def rank_rounds(hcs, exact_ties):
        xs = [sc_s[hc] for hc in hcs]
        ranks = [jnp.full((nk, tm), float(PEER_TOPK), F32) for _ in hcs]
        for r in range(PEER_TOPK):
            for n, hc in enumerate(hcs):
                if exact_ties:
                    mx, hit = _first_max_onehot(xs[n], key_f, 0)
                else:
                    mx = jnp.max(xs[n], axis=0, keepdims=True)
                    hit = xs[n] == mx
                ranks[n] = jnp.where(hit, float(r), ranks[n])
                xs[n] = jnp.where(hit, -jnp.inf, xs[n])
                top_s[hc, r] = jnp.broadcast_to(mx, (8, tm))
        for n, hc in enumerate(hcs):
            rank_s[hc] = ranks[n]
        return ranks

    def rank_body(h, carry):
        hcs = (2 * h, 2 * h + 1)
        ranks = rank_rounds(hcs, exact_ties=False)
        ranked = sum(jnp.sum(jnp.where(rk < float(PEER_TOPK), 1.0, 0.0), axis=0, keepdims=True) for rk in ranks)
        tied = jnp.max(jnp.abs(ranked - float(len(hcs) * PEER_TOPK))) > 0.0

        @pl.when(tied)
        def _():
            rank_rounds(hcs, exact_ties=True)

        return carry

    lax.fori_loop(0, PEER_HEADS, rank_body, 0)

    def heads_on_sublanes(side, r):
        out = top_s[side, r]
        for h in range(1, PEER_HEADS):
            out = jnp.where(sub == h, top_s[2 * h + side, r], out)
        return out

    a = [heads_on_sublanes(0, r) for r in range(PEER_TOPK)]
    bb = [heads_on_sublanes(1, r) for r in range(PEER_TOPK)]
    cur0 = tuple(a[i] + bb[j] for (i, j) in cells)
    best = cur0[0]

    def pick_body(_, carry):
        cur, taken = carry
        mx = functools.reduce(jnp.maximum, cur)
        first = functools.reduce(jnp.minimum, [jnp.where(cur[ci] == mx, float(ci), jnp.inf)
                                               for ci in range(len(cells))])
        hits = [first == float(ci) for ci in range(len(cells))]
        return (tuple(jnp.where(hit, -jnp.inf, c) for hit, c in zip(hits, cur)),
                tuple(jnp.where(hit, 1.0, t) for hit, t in zip(hits, taken)))

    state = (cur0, tuple(jnp.zeros_like(best) for _ in cells))
    for r in range(PEER_TOPK):
        state = pick_body(r, state)
    taken = state[1]
    z = jnp.zeros_like(best)
    counts = [jnp.zeros_like(best) for _ in range(PEER_TOPK)]
    for ci, (i, j) in enumerate(cells):
        counts[i] = counts[i] + taken[ci]
        z = z + taken[ci] * jnp.exp(a[i] + bb[j] - best)
    for h in range(PEER_HEADS):
        z_s[h] = jnp.broadcast_to(z[h:h + 1], (8, tm))
        for r in range(PEER_TOPK):
            cnt_s[h, r] = jnp.broadcast_to(counts[r][h:h + 1], (8, tm))

    def gate_body(h, carry):
        r1 = rank_s[2 * h]
        nc = jnp.zeros((nk, tm), F32)
        for r in range(PEER_TOPK):
            nc = jnp.where(r1 == float(r), cnt_s[h, r, 0:1, :], nc)
        nc1_ref[h] = nc
        r2_ref[h] = rank_s[2 * h + 1].astype(r2_ref.dtype)
        a0 = top_s[2 * h, 0, 0:1, :]
        b0 = top_s[2 * h + 1, 0, 0:1, :]
        e1_ref[h] = jnp.exp(sc_s[2 * h] - a0) / z_s[h, 0:1, :]
        e2_ref[h] = jnp.exp(sc_s[2 * h + 1] - b0).astype(e2_ref.dtype)
        return carry

    lax.fori_loop(0, PEER_HEADS, gate_body, 0)


def _peer_route(h1, g_ffn, wq_bf, keys_bf, *, tm):
    t = h1.shape[0]
    assert t % tm == 0 and tm % LANES == 0
    const = lambda shape: pl.BlockSpec(shape, lambda i: (0,) * len(shape))
    gate_shape = lambda dt: jax.ShapeDtypeStruct((PEER_HEADS, PEER_NKEYS, t), dt)
    gate_spec = pl.BlockSpec((PEER_HEADS, PEER_NKEYS, tm), lambda i: (0, 0, i))
    return pl.pallas_call(
        _route_kernel,
        grid=(t // tm,),
        in_specs=[pl.BlockSpec((tm, D_MODEL), lambda i: (i, 0)), const((1, D_MODEL)),
                  const(wq_bf.shape), const(keys_bf.shape)],
        out_specs=[pl.BlockSpec((tm, D_MODEL), lambda i: (i, 0))] + [gate_spec] * 4,
        out_shape=[jax.ShapeDtypeStruct((t, D_MODEL), BF16),
                   gate_shape(F32), gate_shape(BF16), gate_shape(F32), gate_shape(BF16)],
        scratch_shapes=[pltpu.VMEM((2 * PEER_HEADS, PEER_NKEYS, tm), F32),
                        pltpu.VMEM((2 * PEER_HEADS, PEER_NKEYS, tm), F32),
                        pltpu.VMEM((2 * PEER_HEADS, PEER_TOPK, 8, tm), F32),
                        pltpu.VMEM((PEER_HEADS, PEER_TOPK, 8, tm), F32),
                        pltpu.VMEM((PEER_HEADS, 8, tm), F32)],
        compiler_params=_cparams(1), name="route",
    )(h1, g_ffn, wq_bf, keys_bf)


def _experts_kernel(xn_ref, u_ref, vt_ref, nc1_ref, r2_ref, e1_ref, e2_ref, h_ref, o_ref, acc, w_s):
    c = pl.program_id(1)
    last = pl.num_programs(1) - 1

    @pl.when(c == 0)
    def _():
        acc[...] = jnp.zeros_like(acc)
        w_s[...] = jnp.zeros_like(w_s)

    acc[...] += jnp.dot(vt_ref[...], w_s[...], preferred_element_type=F32)

    blk = jnp.minimum(c, last - 1)
    pre = _dot_nt(u_ref[...], xn_ref[...])
    act = (0.5 * pre * (1.0 + lax.erf(pre * (2.0 ** -0.5)))).astype(BF16)
    n_i1 = EXPERT_BLOCK // PEER_NKEYS
    zero = jnp.zeros((PEER_NKEYS, act.shape[1]), BF16)
    gates = [zero] * n_i1
    for h in range(PEER_HEADS):
        r2 = r2_ref[h]
        e2 = e2_ref[h]
        for u in range(n_i1):
            i1 = blk * n_i1 + u
            nc = nc1_ref[h, pl.ds(i1, 1), :].astype(BF16)
            e1 = e1_ref[h, pl.ds(i1, 1), :].astype(BF16)
            gates[u] = gates[u] + jnp.where(r2 < nc, e2, zero) * e1
    w_s[...] = jnp.concatenate(gates, axis=0) * act

    @pl.when(c == last)
    def _():
        o_ref[...] = h_ref[...] + acc[...].T


def _peer_experts(xn, u_bf, vt_bf, nc1, r2, e1, e2, h1, *, tm):
    t = xn.shape[0]
    n_exp = u_bf.shape[0]
    assert t % tm == 0 and n_exp % EXPERT_BLOCK == 0
    n_blk = n_exp // EXPERT_BLOCK
    once = dict(pipeline_mode=pl.Buffered(1))
    gate_spec = pl.BlockSpec((PEER_HEADS, PEER_NKEYS, tm), lambda i, c: (0, 0, i), **once)
    tok_spec = pl.BlockSpec((tm, D_MODEL), lambda i, c: (i, 0))
    return pl.pallas_call(
        _experts_kernel,
        grid=(t // tm, n_blk + 1),
        in_specs=[tok_spec,
                  pl.BlockSpec((EXPERT_BLOCK, D_MODEL), lambda i, c: (jnp.minimum(c, n_blk - 1), 0)),
                  pl.BlockSpec((D_MODEL, EXPERT_BLOCK), lambda i, c: (0, jnp.maximum(c - 1, 0))),
                  gate_spec, gate_spec, gate_spec, gate_spec,
                  pl.BlockSpec((tm, D_MODEL), lambda i, c: (i, 0), **once)],
        out_specs=tok_spec,
        out_shape=jax.ShapeDtypeStruct((t, D_MODEL), F32),
        scratch_shapes=[pltpu.VMEM((D_MODEL, tm), F32), pltpu.VMEM((EXPERT_BLOCK, tm), BF16)],
        compiler_params=_cparams(2), name="experts",
    )(xn, u_bf, vt_bf, nc1, r2, e1, e2, h1)


def _ple_kernel(h_ref, p_ref, gple_ref, wg_ref, wp_ref, gfin_ref, y_ref):
    h = h_ref[...]
    gate = _sigmoid(jnp.dot(_rmsnorm(h, gple_ref[...]).astype(BF16), wg_ref[...], preferred_element_type=F32))
    proj = jnp.dot(p_ref[...].astype(BF16), wp_ref[...], preferred_element_type=F32)
    y_ref[...] = _rmsnorm(h + gate * proj, gfin_ref[...])


def _ple_final(h2, p_emb, g_ple, wg_bf, wp_bf, g_final, *, tm):
    t = h2.shape[0]
    const = lambda shape: pl.BlockSpec(shape, lambda i: (0,) * len(shape))
    return pl.pallas_call(
        _ple_kernel,
        grid=(t // tm,),
        in_specs=[pl.BlockSpec((tm, D_MODEL), lambda i: (i, 0)), pl.BlockSpec((tm, PLE_DIM), lambda i: (i, 0)),
                  const((1, D_MODEL)), const((D_MODEL, D_MODEL)), const((PLE_DIM, D_MODEL)), const((1, D_MODEL))],
        out_specs=pl.BlockSpec((tm, D_MODEL), lambda i: (i, 0)),
        out_shape=jax.ShapeDtypeStruct((t, D_MODEL), F32),
        compiler_params=_cparams(1), name="ple",
    )(h2, p_emb, g_ple, wg_bf, wp_bf, g_final)


def _pick_tile(t, want):
    tm = min(want, t)
    assert t % tm == 0
    return tm


def _ffn(h1, p_emb, lw, g_final):
    t = h1.shape[0]
    xn, nc1, r2, e1, e2 = _peer_route(h1, lw["g_ffn"], lw["wq"], lw["keys"], tm=_pick_tile(t, 256))
    h2 = _peer_experts(xn, lw["u"], lw["vt"], nc1, r2, e1, e2, h1, tm=_pick_tile(t, 1024))
    return _ple_final(h2, p_emb, lw["g_ple"], lw["w_ple_gate"], lw["w_ple_proj"], g_final, tm=_pick_tile(t, 256))


def kernel(x_prompt, x_sample, cache_k, cache_v, cache_conv, page_table, p_prompt, p_sample,
           g_mix, w_in, w_dw, b_dw, ln_g, ln_b, w_out, g_ffn, peer_wq, peer_keys, peer_u, peer_v,
           g_ple, w_ple_gate, w_ple_proj, g_final):
    depth = g_mix.shape[0]
    batch, seq, _ = x_prompt.shape
    dec_batch, dec_seq, _ = x_sample.shape
    assert depth == 1 and batch == 1 and dec_seq == 1
    past_len = page_table.shape[1] * PAGE_SIZE
    assert seq % MOBA_BLOCK == 0 and past_len % MOBA_BLOCK == 0 and seq >= CONV_K - 1
    row = lambda a: a.reshape(1, -1)
    lane = jnp.arange(LANES) % (ROT_DIM // 2)
    inv_lane = (ROPE_THETA ** (-lane.astype(F32) * 2.0 / ROT_DIM)).reshape(1, LANES)
    l = 0
    lw = dict(
        g_ffn=row(g_ffn[l]), wq=peer_wq[l].astype(BF16),
        keys=peer_keys[l].reshape(2 * PEER_HEADS, PEER_NKEYS, -1).astype(BF16),
        u=peer_u[l].astype(BF16), vt=peer_v[l].T.astype(BF16),
        g_ple=row(g_ple[l]), w_ple_gate=w_ple_gate[l].astype(BF16), w_ple_proj=w_ple_proj[l].astype(BF16))
    w_in_bf = w_in[l].astype(BF16)
    w_out_bf = w_out[l].astype(BF16)
    w_dw_pad = jnp.pad(w_dw[l], ((0, CONV_HALO - CONV_K), (0, 0)))
    conv_args = (w_dw_pad, row(b_dw[l]), row(ln_g[l]), row(ln_b[l]), w_out_bf)
    g_fin = row(g_final)

    xp = x_prompt.reshape(seq, D_MODEL)
    k_p, v_p, glu_p, qp, kp, vt, kmean = _inproj(xp, row(g_mix[l]), w_in_bf, inv_lane, pos0=0, pos_stride=1,
                                                 tm=MOBA_BLOCK, with_aux=True)
    attn_p = _moba_prompt(qp, kp, vt, kmean)
    h1_p = _mix_prompt(glu_p, attn_p, xp, *conv_args, tm=_pick_tile(seq, 256))
    y_p = _ffn(h1_p, p_prompt[l].reshape(seq, PLE_DIM), lw, g_fin)

    xs = x_sample.reshape(dec_batch, D_MODEL)
    k_s, v_s, glu_s, q_s = _inproj(xs, row(g_mix[l]), w_in_bf, inv_lane, pos0=past_len, pos_stride=0,
                                   tm=dec_batch, with_aux=False)
    attn_s = _moba_decode(q_s, k_s, v_s, cache_k[l], cache_v[l], page_table)
    xin_s = jnp.concatenate([cache_conv[l], glu_s[:, None, :]], axis=1)
    xin_pad = jnp.pad(xin_s, ((0, 0), (0, CONV_HALO - CONV_K), (0, 0)))
    h1_s = _mix_decode(xin_pad, attn_s, xs, *conv_args)
    pad = (-dec_batch) % LANES
    h1_s = jnp.pad(h1_s, ((0, pad), (0, 0)))
    p_s = jnp.pad(p_sample[l].reshape(dec_batch, PLE_DIM), ((0, pad), (0, 0)))
    y_s = _ffn(h1_s, p_s, lw, g_fin)[:dec_batch]

    conv_p = glu_p[seq - (CONV_K - 1):].reshape(1, 1, CONV_K - 1, CONV_CH)
    conv_s = xin_s[:, 1:].reshape(1, dec_batch, CONV_K - 1, CONV_CH)
    return (y_p.reshape(1, seq, D_MODEL), y_s.reshape(dec_batch, 1, D_MODEL),
            k_p.reshape(1, 1, seq, N_HEADS, HEAD_DIM), v_p.reshape(1, 1, seq, N_HEADS, HEAD_DIM), conv_p,
            k_s.reshape(1, dec_batch, 1, N_HEADS, HEAD_DIM), v_s.reshape(1, dec_batch, 1, N_HEADS, HEAD_DIM),
            conv_s)
```

```python
import functools

import jax
import jax.numpy as jnp
from jax import lax
from jax.experimental import pallas as pl
from jax.experimental.pallas import tpu as pltpu

F32 = jnp.float32
BF16 = jnp.bfloat16

D_MODEL = 1024
N_HEADS = 8
HEAD_DIM = 64
ATT_WIDTH = N_HEADS * HEAD_DIM
CONV_CH = D_MODEL - ATT_WIDTH
ROT_DIM = HEAD_DIM // 4
ROPE_THETA = 500000.0
MOBA_BLOCK = 256
MOBA_TOPK = 3
CONV_K = 31
PAGE_SIZE = 128
PEER_HEADS = 8
PEER_NKEYS = 128
PEER_TOPK = 16
PLE_DIM = 256
NORM_EPS = 1e-6

LANES = 128
SUBLANES = 8
CONV_HALO = 32
MASKED = -1e30
EXPERT_BLOCK = 4 * PEER_NKEYS
VMEM_LIMIT = 56 * 1024 * 1024
LOG2_E = 1.4426950408889634


def _cparams(n_axes):
    return pltpu.CompilerParams(dimension_semantics=("arbitrary",) * n_axes, vmem_limit_bytes=VMEM_LIMIT)


def _rmsnorm(x, g):
    return x * lax.rsqrt(jnp.mean(x * x, axis=-1, keepdims=True) + NORM_EPS) * g


def _sigmoid(x):
    return 1.0 / (1.0 + jnp.exp(-x))


def _dot_nt(a, b):
    return lax.dot_general(a, b, (((1,), (1,)), ((), ())), preferred_element_type=F32)


def _rope(x, cos4, sin4, low4):
    half = ROT_DIM // 2
    from_hi = pltpu.roll(x, ATT_WIDTH - half, 1)
    from_lo = pltpu.roll(x, half, 1)
    return x * cos4 + jnp.where(low4, from_hi, from_lo) * sin4


def _inproj_kernel(pos0, pos_stride, with_aux, x_ref, g_ref, w_ref, inv_ref, *out_refs):
    if with_aux:
        k_ref, v_ref, glu_ref, q_ref, kp_ref, vt_ref, kmean_ref = out_refs
    else:
        k_ref, v_ref, glu_ref, q_ref = out_refs
    tm = x_ref.shape[0]
    xn = _rmsnorm(x_ref[...], g_ref[...]).astype(BF16)
    z = jnp.dot(xn, w_ref[...], preferred_element_type=F32)
    q = z[:, 0 * ATT_WIDTH:1 * ATT_WIDTH]
    k = z[:, 1 * ATT_WIDTH:2 * ATT_WIDTH]
    v = z[:, 2 * ATT_WIDTH:3 * ATT_WIDTH]
    ga = z[:, 3 * ATT_WIDTH:3 * ATT_WIDTH + CONV_CH]
    gb = z[:, 3 * ATT_WIDTH + CONV_CH:]

    row = lax.broadcasted_iota(jnp.int32, (tm, LANES), 0)
    d = lax.broadcasted_iota(jnp.int32, (tm, LANES), 1) % HEAD_DIM
    pos = (pos0 + (pl.program_id(0) * tm + row) * pos_stride).astype(F32)
    ang = pos * inv_ref[...]
    cos = jnp.where(d < ROT_DIM, jnp.cos(ang), 1.0)
    sin = jnp.sin(ang)
    sin = jnp.where(d < ROT_DIM // 2, -sin, jnp.where(d < ROT_DIM, sin, 0.0))
    reps = ATT_WIDTH // LANES
    cos4 = jnp.concatenate([cos] * reps, axis=1)
    sin4 = jnp.concatenate([sin] * reps, axis=1)
    low4 = lax.broadcasted_iota(jnp.int32, (tm, ATT_WIDTH), 1) % HEAD_DIM < ROT_DIM // 2

    k = _rope(k, cos4, sin4, low4)
    q = _rope(q, cos4, sin4, low4) * (HEAD_DIM ** -0.5)
    k_ref[...] = k
    v_ref[...] = v
    glu_ref[...] = ga * _sigmoid(gb)
    if with_aux:
        q_ref[...] = (q * LOG2_E).astype(BF16)
        kp_ref[...] = k.astype(BF16)
        kmean_ref[0] = jnp.mean(k, axis=0, keepdims=True)
        vt = v.T
        for h in range(N_HEADS):
            vt_ref[h, 0] = vt[h * HEAD_DIM:(h + 1) * HEAD_DIM, :].astype(BF16)
    else:
        q_ref[...] = q


def _inproj(x, g_mix, w_in_bf, inv_lane, *, pos0, pos_stride, tm, with_aux):
    t = x.shape[0]
    n_in = w_in_bf.shape[1]
    grid = (t // tm,)
    row_spec = lambda w: pl.BlockSpec((tm, w), lambda i: (i, 0))
    const = lambda shape: pl.BlockSpec(shape, lambda i: (0,) * len(shape))
    out_shape = [jax.ShapeDtypeStruct((t, ATT_WIDTH), F32), jax.ShapeDtypeStruct((t, ATT_WIDTH), F32),
                 jax.ShapeDtypeStruct((t, CONV_CH), F32)]
    out_specs = [row_spec(ATT_WIDTH), row_spec(ATT_WIDTH), row_spec(CONV_CH)]
    if with_aux:
        assert tm == MOBA_BLOCK
        nb = t // MOBA_BLOCK
        out_shape += [jax.ShapeDtypeStruct((t, ATT_WIDTH), BF16), jax.ShapeDtypeStruct((t, ATT_WIDTH), BF16),
                      jax.ShapeDtypeStruct((N_HEADS, nb, HEAD_DIM, MOBA_BLOCK), BF16),
                      jax.ShapeDtypeStruct((nb, 1, ATT_WIDTH), F32)]
        out_specs += [row_spec(ATT_WIDTH), row_spec(ATT_WIDTH),
                      pl.BlockSpec((N_HEADS, 1, HEAD_DIM, MOBA_BLOCK), lambda i: (0, i, 0, 0)),
                      pl.BlockSpec((1, 1, ATT_WIDTH), lambda i: (i, 0, 0))]
    else:
        out_shape += [jax.ShapeDtypeStruct((t, ATT_WIDTH), F32)]
        out_specs += [row_spec(ATT_WIDTH)]
    return pl.pallas_call(
        functools.partial(_inproj_kernel, pos0, pos_stride, with_aux),
        grid=grid,
        in_specs=[row_spec(D_MODEL), const((1, D_MODEL)), const((D_MODEL, n_in)), const((1, LANES))],
        out_specs=out_specs, out_shape=out_shape, compiler_params=_cparams(1),
        name="inproj_aux" if with_aux else "inproj",
    )(x, g_mix, w_in_bf, inv_lane)


def _first_max_onehot(g, idx_f, axis):
    mx = jnp.max(g, axis=axis, keepdims=True)
    first = jnp.min(jnp.where(g == mx, idx_f, jnp.inf), axis=axis, keepdims=True)
    return mx, idx_f == first


def _moba_kernel(chunk, q_ref, kp_ref, vt_ref, kmean_ref, o_ref):
    i = pl.program_id(1)
    blk = MOBA_BLOCK
    q2 = q_ref[...]
    km = kmean_ref[...].astype(BF16)
    lane = lax.broadcasted_iota(jnp.int32, (blk, LANES), 1)
    lane_f = lane.astype(F32)
    key_row = lax.broadcasted_iota(jnp.int32, (blk, blk), 0)
    qry_col = lax.broadcasted_iota(jnp.int32, (blk, blk), 1)

    qzs, biases = [], []
    for hh in range(2):
        qz = jnp.where(lane // HEAD_DIM == hh, q2, jnp.zeros_like(q2))
        gate = _dot_nt(qz, km)
        g = jnp.where(lane < i, gate, -jnp.inf)
        sel = jnp.zeros((blk, LANES), jnp.bool_)
        for _ in range(MOBA_TOPK):
            mx, hit = _first_max_onehot(g, lane_f, 1)
            sel = sel | (hit & (mx > -jnp.inf))
            g = jnp.where(hit, -jnp.inf, g)
        qzs.append(qz)
        biases.append(jnp.where(sel, 0.0, MASKED))

    k_own = kp_ref[pl.ds(pl.multiple_of(i * blk, blk), blk), :]
    state = []
    for hh in range(2):
        s = _dot_nt(k_own, qzs[hh])
        s = jnp.where(key_row <= qry_col, s, MASKED)
        m = jnp.max(s, axis=0, keepdims=True)
        p = jnp.exp2(s - m)
        l = jnp.sum(p, axis=0, keepdims=True)
        acc = jnp.dot(vt_ref[hh, i], p.astype(BF16), preferred_element_type=F32)
        state += [m, l, acc]

    def body(c, carry):
        base = c * chunk
        shift = lax.rem(LANES - base, LANES)
        k_augs = []
        for r in range(chunk):
            kr = kp_ref[pl.ds(pl.multiple_of((base + r) * blk, blk), blk), :]
            k_augs.append(jnp.concatenate([kr, (lane == r).astype(BF16)], axis=1))
        out = []
        for hh in range(2):
            m, l, acc = carry[3 * hh:3 * hh + 3]
            bias_c = pltpu.roll(biases[hh], shift, 1).astype(BF16)
            q_aug = jnp.concatenate([qzs[hh], bias_c], axis=1)
            ss = [_dot_nt(ka, q_aug) for ka in k_augs]
            mc = functools.reduce(jnp.maximum, [jnp.max(s, axis=0, keepdims=True) for s in ss])
            m_new = jnp.maximum(m, mc)
            alpha = jnp.exp2(m - m_new)
            l = alpha * l
            acc = alpha * acc
            for r in range(chunk):
                p = jnp.exp2(ss[r] - m_new)
                l = l + jnp.sum(p, axis=0, keepdims=True)
                acc = acc + jnp.dot(vt_ref[hh, base + r], p.astype(BF16), preferred_element_type=F32)
            out += [m_new, l, acc]
        return tuple(out)

    state = lax.fori_loop(0, (i + chunk - 1) // chunk, body, tuple(state))
    o_t = jnp.concatenate([state[2] / state[1], state[5] / state[4]], axis=0)
    o_ref[...] = o_t.T.astype(o_ref.dtype)


def _moba_prompt(qp, kp, vt, kmean):
    t = qp.shape[0]
    nb = t // MOBA_BLOCK
    assert t % MOBA_BLOCK == 0 and nb <= LANES
    chunk = next(c for c in (8, 4, 2, 1) if nb % c == 0)
    km = jnp.pad(kmean.reshape(nb, ATT_WIDTH), ((0, LANES - nb), (0, 0)))
    return pl.pallas_call(
        functools.partial(_moba_kernel, chunk),
        grid=(N_HEADS // 2, nb),
        in_specs=[pl.BlockSpec((MOBA_BLOCK, LANES), lambda hp, i: (i, hp)),
                  pl.BlockSpec((t, LANES), lambda hp, i: (0, hp)),
                  pl.BlockSpec((2, nb, HEAD_DIM, MOBA_BLOCK), lambda hp, i: (hp, 0, 0, 0)),
                  pl.BlockSpec((LANES, LANES), lambda hp, i: (0, hp))],
        out_specs=pl.BlockSpec((MOBA_BLOCK, LANES), lambda hp, i: (i, hp)),
        out_shape=jax.ShapeDtypeStruct((t, ATT_WIDTH), BF16),
        compiler_params=_cparams(2), name="moba",
    )(qp, kp, vt, km)


def _dec_scores_kernel(bps, pt_ref, qb_ref, *refs):
    del pt_ref
    k_refs = refs[:2 * bps]
    s_ref, sel_ref, gate_s = refs[2 * bps:]
    step = pl.program_id(1)
    blk_lane = lax.broadcasted_iota(jnp.int32, (N_HEADS, LANES), 1)

    @pl.when(step == 0)
    def _():
        gate_s[...] = jnp.full_like(gate_s, -jnp.inf)

    qb = qb_ref[0]
    for bi in range(bps):
        s = jnp.concatenate([jnp.sum(k_refs[2 * bi + w][0] * qb, axis=1) for w in range(2)], axis=1)
        s_ref[0, :, bi * MOBA_BLOCK:(bi + 1) * MOBA_BLOCK] = s
        gate_s[...] = jnp.where(blk_lane == step * bps + bi, jnp.mean(s, axis=1, keepdims=True), gate_s[...])

    @pl.when(step == pl.num_programs(1) - 1)
    def _():
        g = gate_s[...]
        lane_f = blk_lane.astype(F32)
        out = jnp.full((N_HEADS, LANES), -1.0, F32)
        for r in range(MOBA_TOPK):
            mx, hit = _first_max_onehot(g, lane_f, 1)
            idx = jnp.min(jnp.where(hit, lane_f, jnp.inf), axis=1, keepdims=True)
            out = jnp.where((blk_lane == r) & (mx > -jnp.inf), idx, out)
            g = jnp.where(hit, -jnp.inf, g)
        sel_ref[0] = out


def _dec_combine_kernel(pt_ref, sel_ref, q_ref, kn_ref, vn_ref, *refs):
    del pt_ref
    s_refs = refs[:MOBA_TOPK]
    v_refs = refs[MOBA_TOPK:3 * MOBA_TOPK]
    o_ref = refs[3 * MOBA_TOPK]
    b, h = pl.program_id(0), pl.program_id(1)
    valid = [sel_ref[b, MOBA_TOPK * h + r] >= 0 for r in range(MOBA_TOPK)]
    s_own = jnp.sum(kn_ref[0] * q_ref[0], axis=1, keepdims=True)
    scores = [jnp.where(valid[r], s_refs[r][0], -jnp.inf) for r in range(MOBA_TOPK)]
    big = functools.reduce(jnp.maximum, [jnp.max(sc, axis=1, keepdims=True) for sc in scores] + [s_own])
    w_own = jnp.exp(s_own - big)
    den = w_own
    num = w_own * vn_ref[0]
    for r in range(MOBA_TOPK):
        p = jnp.exp(scores[r] - big)
        den = den + jnp.sum(p, axis=1, keepdims=True)
        pv = sum(v_refs[2 * r + w][0] * p[:, None, w * PAGE_SIZE:(w + 1) * PAGE_SIZE] for w in range(2))
        num = num + jnp.sum(pv, axis=2)
    o_ref[0] = num / den


def _moba_decode(q_s, k_new, v_new, cache_k, cache_v, page_table):
    b = q_s.shape[0]
    n_pages = page_table.shape[1]
    pages_per_blk = MOBA_BLOCK // PAGE_SIZE
    assert pages_per_blk == 2 and n_pages % pages_per_blk == 0
    nbp = n_pages // pages_per_blk
    assert nbp <= LANES
    bps = next(c for c in (8, 4, 2, 1) if nbp % c == 0)
    past = nbp * MOBA_BLOCK
    ck = jnp.transpose(cache_k, (0, 2, 3, 1))
    cv = jnp.transpose(cache_v, (0, 2, 3, 1))
    q_lanes = jnp.broadcast_to(q_s.reshape(b, N_HEADS, HEAD_DIM)[..., None], (b, N_HEADS, HEAD_DIM, LANES))
    page = lambda j: pl.BlockSpec((1, N_HEADS, HEAD_DIM, PAGE_SIZE),
                                  lambda s, n, pt: (pt[s, pages_per_blk * bps * n + j], 0, 0, 0))
    pages = [page(j) for j in range(pages_per_blk * bps)]
    scores, sel = pl.pallas_call(
        functools.partial(_dec_scores_kernel, bps),
        grid_spec=pltpu.PrefetchScalarGridSpec(
            num_scalar_prefetch=1, grid=(b, nbp // bps),
            in_specs=[pl.BlockSpec((1, N_HEADS, HEAD_DIM, LANES), lambda s, n, pt: (s, 0, 0, 0))] + pages,
            out_specs=[pl.BlockSpec((1, N_HEADS, bps * MOBA_BLOCK), lambda s, n, pt: (s, 0, n)),
                       pl.BlockSpec((1, N_HEADS, LANES), lambda s, n, pt: (s, 0, 0))],
            scratch_shapes=[pltpu.VMEM((N_HEADS, LANES), F32)]),
        out_shape=[jax.ShapeDtypeStruct((b, N_HEADS, past), F32), jax.ShapeDtypeStruct((b, N_HEADS, LANES), F32)],
        compiler_params=_cparams(2), name="dec_scores",
    )(page_table, q_lanes, *([ck] * len(pages)))

    sel_i = sel[:, :, :MOBA_TOPK].astype(jnp.int32).reshape(b, N_HEADS * MOBA_TOPK)
    rows = lambda a: a.reshape(b * N_HEADS, 1, HEAD_DIM)
    pick = lambda s, h, sl, r: jnp.maximum(sl[s, MOBA_TOPK * h + r], 0)
    row_spec = pl.BlockSpec((1, 1, HEAD_DIM), lambda s, h, pt, sl: (s * N_HEADS + h, 0, 0))
    score_spec = lambda r: pl.BlockSpec((1, 1, MOBA_BLOCK),
                                        lambda s, h, pt, sl: (s * N_HEADS + h, 0, pick(s, h, sl, r)))
    value_spec = lambda r, w: pl.BlockSpec(
        (1, 1, HEAD_DIM, PAGE_SIZE), lambda s, h, pt, sl: (pt[s, pages_per_blk * pick(s, h, sl, r) + w], h, 0, 0))
    value_specs = [value_spec(r, w) for r in range(MOBA_TOPK) for w in range(pages_per_blk)]
    out = pl.pallas_call(
        _dec_combine_kernel,
        grid_spec=pltpu.PrefetchScalarGridSpec(
            num_scalar_prefetch=2, grid=(b, N_HEADS),
            in_specs=[row_spec, row_spec, row_spec] + [score_spec(r) for r in range(MOBA_TOPK)] + value_specs,
            out_specs=row_spec),
        out_shape=jax.ShapeDtypeStruct((b * N_HEADS, 1, HEAD_DIM), F32),
        compiler_params=_cparams(2), name="dec_combine",
    )(page_table, sel_i, rows(q_s), rows(k_new), rows(v_new),
      *([scores.reshape(b * N_HEADS, 1, past)] * MOBA_TOPK), *([cv] * len(value_specs)))
    return out.reshape(b, ATT_WIDTH)


def _post_conv(y, attn_bf, x, b_dw, ln_g, ln_b, w_out):
    y = y + b_dw
    mu = jnp.mean(y, axis=-1, keepdims=True)
    yc = y - mu
    var = jnp.mean(yc * yc, axis=-1, keepdims=True)
    y = yc * lax.rsqrt(var + NORM_EPS) * ln_g + ln_b
    y = y * _sigmoid(y)
    cat = jnp.concatenate([attn_bf, y.astype(BF16)], axis=1)
    return x + jnp.dot(cat, w_out, preferred_element_type=F32)


def _mix_kernel(glu_ref, halo_ref, attn_ref, x_ref, wdw_ref, bdw_ref, lng_ref, lnb_ref, wout_ref, h_ref, xin):
    tm = glu_ref.shape[0]
    first = pl.program_id(0) == 0
    xin[0:CONV_HALO, :] = jnp.where(first, 0.0, halo_ref[...])
    xin[CONV_HALO:CONV_HALO + tm, :] = glu_ref[...]
    xin[CONV_HALO + tm:, :] = jnp.zeros((SUBLANES, CONV_CH), F32)
    off = CONV_HALO - (CONV_K - 1)
    y = jnp.zeros((tm, CONV_CH), F32)
    for b in range(SUBLANES):
        z = None
        for a in range((CONV_HALO + SUBLANES) // SUBLANES):
            k = SUBLANES * a + b - off
            if 0 <= k < CONV_K:
                term = wdw_ref[k:k + 1, :] * xin[SUBLANES * a:SUBLANES * a + tm + SUBLANES, :]
                z = term if z is None else z + term
        if z is not None:
            y = y + z[b:b + tm]
    h_ref[...] = _post_conv(y, attn_ref[...], x_ref[...], bdw_ref[...], lng_ref[...], lnb_ref[...], wout_ref[...])


def _mix_prompt(glu, attn_bf, x, w_dw, b_dw, ln_g, ln_b, w_out_bf, *, tm):
    t = glu.shape[0]
    assert tm % CONV_HALO == 0 and t % tm == 0
    ratio = tm // CONV_HALO
    const = lambda shape: pl.BlockSpec(shape, lambda i: (0,) * len(shape))
    return pl.pallas_call(
        _mix_kernel,
        grid=(t // tm,),
        in_specs=[pl.BlockSpec((tm, CONV_CH), lambda i: (i, 0)),
                  pl.BlockSpec((CONV_HALO, CONV_CH), lambda i: (jnp.maximum(i * ratio - 1, 0), 0)),
                  pl.BlockSpec((tm, ATT_WIDTH), lambda i: (i, 0)),
                  pl.BlockSpec((tm, D_MODEL), lambda i: (i, 0)),
                  const((CONV_HALO, CONV_CH)), const((1, CONV_CH)), const((1, CONV_CH)), const((1, CONV_CH)),
                  const((D_MODEL, D_MODEL))],
        out_specs=pl.BlockSpec((tm, D_MODEL), lambda i: (i, 0)),
        out_shape=jax.ShapeDtypeStruct((t, D_MODEL), F32),
        scratch_shapes=[pltpu.VMEM((tm + CONV_HALO + SUBLANES, CONV_CH), F32)],
        compiler_params=_cparams(1), name="mix",
    )(glu, glu, attn_bf, x, w_dw, b_dw, ln_g, ln_b, w_out_bf)


def _mix_dec_kernel(xin_ref, attn_ref, x_ref, wdw_ref, bdw_ref, lng_ref, lnb_ref, wout_ref, h_ref):
    y = jnp.sum(xin_ref[...] * wdw_ref[...][None], axis=1)
    h_ref[...] = _post_conv(y, attn_ref[...].astype(BF16), x_ref[...], bdw_ref[...], lng_ref[...],
                            lnb_ref[...], wout_ref[...])


def _mix_decode(xin, attn, x, w_dw, b_dw, ln_g, ln_b, w_out_bf):
    b = x.shape[0]
    full = lambda a: pl.BlockSpec(a.shape, lambda i: (0,) * a.ndim)
    args = (xin, attn, x, w_dw, b_dw, ln_g, ln_b, w_out_bf)
    return pl.pallas_call(
        _mix_dec_kernel, grid=(1,), in_specs=[full(a) for a in args],
        out_specs=pl.BlockSpec((b, D_MODEL), lambda i: (0, 0)),
        out_shape=jax.ShapeDtypeStruct((b, D_MODEL), F32),
        compiler_params=_cparams(1), name="mix_dec",
    )(*args)


def _hyperbola_cells():
    return [(i, j) for i in range(PEER_TOPK) for j in range(PEER_TOPK) if (i + 1) * (j + 1) <= PEER_TOPK]


def _route_kernel(h_ref, g_ref, wq_ref, keys_ref, xn_ref, nc1_ref, r2_ref, e1_ref, e2_ref,
                  sc_s, rank_s, top_s, cnt_s, z_s):
    tm = h_ref.shape[0]
    nk = PEER_NKEYS
    xn = _rmsnorm(h_ref[...], g_ref[...]).astype(BF16)
    xn_ref[...] = xn
    qb = jnp.dot(xn, wq_ref[...], preferred_element_type=F32).astype(BF16)
    for hc in range(2 * PEER_HEADS):
        sc_s[hc] = _dot_nt(keys_ref[hc], qb[:, hc * nk:(hc + 1) * nk])

    key_f = lax.broadcasted_iota(jnp.int32, (nk, tm), 0).astype(F32)
    sub = lax.broadcasted_iota(jnp.int32, (PEER_HEADS, tm), 0)
    cells = _hyperbola_cells()

    def rank_rounds(hcs, exact_ties):
        xs = [sc_s[hc] for hc in hcs]
        ranks = [jnp.full((nk, tm), float(PEER_TOPK), F32) for _ in hcs]
        for r in range(PEER_TOPK):
            for n, hc in enumerate(hcs):
                if exact_ties:
                    mx, hit = _first_max_onehot(xs[n], key_f, 0)
                else:
                    mx = jnp.max(xs[n], axis=0, keepdims=True)
                    hit = xs[n] == mx
                ranks[n] = jnp.where(hit, float(r), ranks[n])
                xs[n] = jnp.where(hit, -jnp.inf, xs[n])
                top_s[hc, r] = jnp.broadcast_to(mx, (8, tm))
        for n, hc in enumerate(hcs):
            rank_s[hc] = ranks[n]
        return ranks

    def rank_body(h, carry):
        hcs = (2 * h, 2 * h + 1)
        ranks = rank_rounds(hcs, exact_ties=False)
        ranked = sum(jnp.sum(jnp.where(rk < float(PEER_TOPK), 1.0, 0.0), axis=0, keepdims=True) for rk in ranks)
        tied = jnp.max(jnp.abs(ranked - float(len(hcs) * PEER_TOPK))) > 0.0

        @pl.when(tied)
        def _():
            rank_rounds(hcs, exact_ties=True)

        return carry

    lax.fori_loop(0, PEER_HEADS, rank_body, 0)

    def heads_on_sublanes(side, r):
        out = top_s[side, r]
        for h in range(1, PEER_HEADS):
            out = jnp.where(sub == h, top_s[2 * h + side, r], out)
        return out

    a = [heads_on_sublanes(0, r) for r in range(PEER_TOPK)]
    bb = [heads_on_sublanes(1, r) for r in range(PEER_TOPK)]
    cur0 = tuple(a[i] + bb[j] for (i, j) in cells)
    best = cur0[0]

    def pick_body(_, carry):
        cur, taken = carry
        mx = functools.reduce(jnp.maximum, cur)
        first = functools.reduce(jnp.minimum, [jnp.where(cur[ci] == mx, float(ci), jnp.inf)
                                               for ci in range(len(cells))])
        hits = [first == float(ci) for ci in range(len(cells))]
        return (tuple(jnp.where(hit, -jnp.inf, c) for hit, c in zip(hits, cur)),
                tuple(jnp.where(hit, 1.0, t) for hit, t in zip(hits, taken)))

    state = (cur0, tuple(jnp.zeros_like(best) for _ in cells))
    for r in range(PEER_TOPK):
        state = pick_body(r, state)
    taken = state[1]
    z = jnp.zeros_like(best)
    counts = [jnp.zeros_like(best) for _ in range(PEER_TOPK)]
    for ci, (i, j) in enumerate(cells):
        counts[i] = counts[i] + taken[ci]
        z = z + taken[ci] * jnp.exp(a[i] + bb[j] - best)
    for h in range(PEER_HEADS):
        z_s[h] = jnp.broadcast_to(z[h:h + 1], (8, tm))
        for r in range(PEER_TOPK):
            cnt_s[h, r] = jnp.broadcast_to(counts[r][h:h + 1], (8, tm))

    def gate_body(h, carry):
        r1 = rank_s[2 * h]
        nc = jnp.zeros((nk, tm), F32)
        for r in range(PEER_TOPK):
            nc = jnp.where(r1 == float(r), cnt_s[h, r, 0:1, :], nc)
        nc1_ref[h] = nc
        r2_ref[h] = rank_s[2 * h + 1].astype(r2_ref.dtype)
        a0 = top_s[2 * h, 0, 0:1, :]
        b0 = top_s[2 * h + 1, 0, 0:1, :]
        e1_ref[h] = jnp.exp(sc_s[2 * h] - a0) / z_s[h, 0:1, :]
        e2_ref[h] = jnp.exp(sc_s[2 * h + 1] - b0).astype(e2_ref.dtype)
        return carry

    lax.fori_loop(0, PEER_HEADS, gate_body, 0)


def _peer_route(h1, g_ffn, wq_bf, keys_bf, *, tm):
    t = h1.shape[0]
    assert t % tm == 0 and tm % LANES == 0
    const = lambda shape: pl.BlockSpec(shape, lambda i: (0,) * len(shape))
    gate_shape = lambda dt: jax.ShapeDtypeStruct((PEER_HEADS, PEER_NKEYS, t), dt)
    gate_spec = pl.BlockSpec((PEER_HEADS, PEER_NKEYS, tm), lambda i: (0, 0, i))
    return pl.pallas_call(
        _route_kernel,
        grid=(t // tm,),
        in_specs=[pl.BlockSpec((tm, D_MODEL), lambda i: (i, 0)), const((1, D_MODEL)),
                  const(wq_bf.shape), const(keys_bf.shape)],
        out_specs=[pl.BlockSpec((tm, D_MODEL), lambda i: (i, 0))] + [gate_spec] * 4,
        out_shape=[jax.ShapeDtypeStruct((t, D_MODEL), BF16),
                   gate_shape(F32), gate_shape(BF16), gate_shape(F32), gate_shape(BF16)],
        scratch_shapes=[pltpu.VMEM((2 * PEER_HEADS, PEER_NKEYS, tm), F32),
                        pltpu.VMEM((2 * PEER_HEADS, PEER_NKEYS, tm), F32),
                        pltpu.VMEM((2 * PEER_HEADS, PEER_TOPK, 8, tm), F32),
                        pltpu.VMEM((PEER_HEADS, PEER_TOPK, 8, tm), F32),
                        pltpu.VMEM((PEER_HEADS, 8, tm), F32)],
        compiler_params=_cparams(1), name="route",
    )(h1, g_ffn, wq_bf, keys_bf)


def _experts_kernel(xn_ref, u_ref, vt_ref, nc1_ref, r2_ref, e1_ref, e2_ref, h_ref, o_ref, acc, w_s):
    c = pl.program_id(1)
    last = pl.num_programs(1) - 1

    @pl.when(c == 0)
    def _():
        acc[...] = jnp.zeros_like(acc)
        w_s[...] = jnp.zeros_like(w_s)

    acc[...] += jnp.dot(vt_ref[...], w_s[...], preferred_element_type=F32)

    blk = jnp.minimum(c, last - 1)
    pre = _dot_nt(u_ref[...], xn_ref[...])
    act = (0.5 * pre * (1.0 + lax.erf(pre * (2.0 ** -0.5)))).astype(BF16)
    n_i1 = EXPERT_BLOCK // PEER_NKEYS
    zero = jnp.zeros((PEER_NKEYS, act.shape[1]), BF16)
    gates = [zero] * n_i1
    for h in range(PEER_HEADS):
        r2 = r2_ref[h]
        e2 = e2_ref[h]
        for u in range(n_i1):
            i1 = blk * n_i1 + u
            nc = nc1_ref[h, pl.ds(i1, 1), :].astype(BF16)
            e1 = e1_ref[h, pl.ds(i1, 1), :].astype(BF16)
            gates[u] = gates[u] + jnp.where(r2 < nc, e2, zero) * e1
    w_s[...] = jnp.concatenate(gates, axis=0) * act

    @pl.when(c == last)
    def _():
        o_ref[...] = h_ref[...] + acc[...].T


def _peer_experts(xn, u_bf, vt_bf, nc1, r2, e1, e2, h1, *, tm):
    t = xn.shape[0]
    n_exp = u_bf.shape[0]
    assert t % tm == 0 and n_exp % EXPERT_BLOCK == 0
    n_blk = n_exp // EXPERT_BLOCK
    once = dict(pipeline_mode=pl.Buffered(1))
    gate_spec = pl.BlockSpec((PEER_HEADS, PEER_NKEYS, tm), lambda i, c: (0, 0, i), **once)
    tok_spec = pl.BlockSpec((tm, D_MODEL), lambda i, c: (i, 0))
    return pl.pallas_call(
        _experts_kernel,
        grid=(t // tm, n_blk + 1),
        in_specs=[tok_spec,
                  pl.BlockSpec((EXPERT_BLOCK, D_MODEL), lambda i, c: (jnp.minimum(c, n_blk - 1), 0)),
                  pl.BlockSpec((D_MODEL, EXPERT_BLOCK), lambda i, c: (0, jnp.maximum(c - 1, 0))),
                  gate_spec, gate_spec, gate_spec, gate_spec,
                  pl.BlockSpec((tm, D_MODEL), lambda i, c: (i, 0), **once)],
        out_specs=tok_spec,
        out_shape=jax.ShapeDtypeStruct((t, D_MODEL), F32),
        scratch_shapes=[pltpu.VMEM((D_MODEL, tm), F32), pltpu.VMEM((EXPERT_BLOCK, tm), BF16)],
        compiler_params=_cparams(2), name="experts",
    )(xn, u_bf, vt_bf, nc1, r2, e1, e2, h1)


def _ple_kernel(h_ref, p_ref, gple_ref, wg_ref, wp_ref, gfin_ref, y_ref):
    h = h_ref[...]
    gate = _sigmoid(jnp.dot(_rmsnorm(h, gple_ref[...]).astype(BF16), wg_ref[...], preferred_element_type=F32))
    proj = jnp.dot(p_ref[...].astype(BF16), wp_ref[...], preferred_element_type=F32)
    y_ref[...] = _rmsnorm(h + gate * proj, gfin_ref[...])


def _ple_final(h2, p_emb, g_ple, wg_bf, wp_bf, g_final, *, tm):
    t = h2.shape[0]
    const = lambda shape: pl.BlockSpec(shape, lambda i: (0,) * len(shape))
    return pl.pallas_call(
        _ple_kernel,
        grid=(t // tm,),
        in_specs=[pl.BlockSpec((tm, D_MODEL), lambda i: (i, 0)), pl.BlockSpec((tm, PLE_DIM), lambda i: (i, 0)),
                  const((1, D_MODEL)), const((D_MODEL, D_MODEL)), const((PLE_DIM, D_MODEL)), const((1, D_MODEL))],
        out_specs=pl.BlockSpec((tm, D_MODEL), lambda i: (i, 0)),
        out_shape=jax.ShapeDtypeStruct((t, D_MODEL), F32),
        compiler_params=_cparams(1), name="ple",
    )(h2, p_emb, g_ple, wg_bf, wp_bf, g_final)


def _pick_tile(t, want):
    tm = min(want, t)
    assert t % tm == 0
    return tm


def _ffn(h1, p_emb, lw, g_final):
    t = h1.shape[0]
    xn, nc1, r2, e1, e2 = _peer_route(h1, lw["g_ffn"], lw["wq"], lw["keys"], tm=_pick_tile(t, 256))
    h2 = _peer_experts(xn, lw["u"], lw["vt"], nc1, r2, e1, e2, h1, tm=_pick_tile(t, 1024))
    return _ple_final(h2, p_emb, lw["g_ple"], lw["w_ple_gate"], lw["w_ple_proj"], g_final, tm=_pick_tile(t, 256))


def kernel(x_prompt, x_sample, cache_k, cache_v, cache_conv, page_table, p_prompt, p_sample,
           g_mix, w_in, w_dw, b_dw, ln_g, ln_b, w_out, g_ffn, peer_wq, peer_keys, peer_u, peer_v,
           g_ple, w_ple_gate, w_ple_proj, g_final):
    depth = g_mix.shape[0]
    batch, seq, _ = x_prompt.shape
    dec_batch, dec_seq, _ = x_sample.shape
    assert depth == 1 and batch == 1 and dec_seq == 1
    past_len = page_table.shape[1] * PAGE_SIZE
    assert seq % MOBA_BLOCK == 0 and past_len % MOBA_BLOCK == 0 and seq >= CONV_K - 1
    row = lambda a: a.reshape(1, -1)
    lane = jnp.arange(LANES) % (ROT_DIM // 2)
    inv_lane = (ROPE_THETA ** (-lane.astype(F32) * 2.0 / ROT_DIM)).reshape(1, LANES)
    l = 0
    lw = dict(
        g_ffn=row(g_ffn[l]), wq=peer_wq[l].astype(BF16),
        keys=peer_keys[l].reshape(2 * PEER_HEADS, PEER_NKEYS, -1).astype(BF16),
        u=peer_u[l].astype(BF16), vt=peer_v[l].T.astype(BF16),
        g_ple=row(g_ple[l]), w_ple_gate=w_ple_gate[l].astype(BF16), w_ple_proj=w_ple_proj[l].astype(BF16))
    w_in_bf = w_in[l].astype(BF16)
    w_out_bf = w_out[l].astype(BF16)
    w_dw_pad = jnp.pad(w_dw[l], ((0, CONV_HALO - CONV_K), (0, 0)))
    conv_args = (w_dw_pad, row(b_dw[l]), row(ln_g[l]), row(ln_b[l]), w_out_bf)
    g_fin = row(g_final)

    xp = x_prompt.reshape(seq, D_MODEL)
    k_p, v_p, glu_p, qp, kp, vt, kmean = _inproj(xp, row(g_mix[l]), w_in_bf, inv_lane, pos0=0, pos_stride=1,
                                                 tm=MOBA_BLOCK, with_aux=True)
    attn_p = _moba_prompt(qp, kp, vt, kmean)
    h1_p = _mix_prompt(glu_p, attn_p, xp, *conv_args, tm=_pick_tile(seq, 256))
    y_p = _ffn(h1_p, p_prompt[l].reshape(seq, PLE_DIM), lw, g_fin)

    xs = x_sample.reshape(dec_batch, D_MODEL)
    k_s, v_s, glu_s, q_s = _inproj(xs, row(g_mix[l]), w_in_bf, inv_lane, pos0=past_len, pos_stride=0,
                                   tm=dec_batch, with_aux=False)
    attn_s = _moba_decode(q_s, k_s, v_s, cache_k[l], cache_v[l], page_table)
    xin_s = jnp.concatenate([cache_conv[l], glu_s[:, None, :]], axis=1)
    xin_pad = jnp.pad(xin_s, ((0, 0), (0, CONV_HALO - CONV_K), (0, 0)))
    h1_s = _mix_decode(xin_pad, attn_s, xs, *conv_args)
    pad = (-dec_batch) % LANES
    h1_s = jnp.pad(h1_s, ((0, pad), (0, 0)))
    p_s = jnp.pad(p_sample[l].reshape(dec_batch, PLE_DIM), ((0, pad), (0, 0)))
    y_s = _ffn(h1_s, p_s, lw, g_fin)[:dec_batch]

    conv_p = glu_p[seq - (CONV_K - 1):].reshape(1, 1, CONV_K - 1, CONV_CH)
    conv_s = xin_s[:, 1:].reshape(1, dec_batch, CONV_K - 1, CONV_CH)
    return (y_p.reshape(1, seq, D_MODEL), y_s.reshape(dec_batch, 1, D_MODEL),
            k_p.reshape(1, 1, seq, N_HEADS, HEAD_DIM), v_p.reshape(1, 1, seq, N_HEADS, HEAD_DIM), conv_p,
            k_s.reshape(1, dec_batch, 1, N_HEADS, HEAD_DIM), v_s.reshape(1, dec_batch, 1, N_HEADS, HEAD_DIM),
            conv_s)
```

```python
import functools

import jax
import jax.numpy as jnp
from jax import lax
from jax.experimental import pallas as pl
from jax.experimental.pallas import tpu as pltpu

F32 = jnp.float32
BF16 = jnp.bfloat16

D_MODEL = 1024
N_HEADS = 8
HEAD_DIM = 64
ATT_WIDTH = N_HEADS * HEAD_DIM
CONV_CH = D_MODEL - ATT_WIDTH
ROT_DIM = HEAD_DIM // 4
ROPE_THETA = 500000.0
MOBA_BLOCK = 256
MOBA_TOPK = 3
CONV_K = 31
PAGE_SIZE = 128
PEER_HEADS = 8
PEER_NKEYS = 128
PEER_TOPK = 16
PLE_DIM = 256
NORM_EPS = 1e-6

LANES = 128
SUBLANES = 8
CONV_HALO = 32
MASKED = -1e30
EXPERT_BLOCK = 4 * PEER_NKEYS
VMEM_LIMIT = 56 * 1024 * 1024
LOG2_E = 1.4426950408889634


def _cparams(n_axes):
    return pltpu.CompilerParams(dimension_semantics=("arbitrary",) * n_axes, vmem_limit_bytes=VMEM_LIMIT)


def _rmsnorm(x, g):
    return x * lax.rsqrt(jnp.mean(x * x, axis=-1, keepdims=True) + NORM_EPS) * g


def _sigmoid(x):
    return 1.0 / (1.0 + jnp.exp(-x))


def _dot_nt(a, b):
    return lax.dot_general(a, b, (((1,), (1,)), ((), ())), preferred_element_type=F32)


def _rope(x, cos4, sin4, low4):
    half = ROT_DIM // 2
    from_hi = pltpu.roll(x, ATT_WIDTH - half, 1)
    from_lo = pltpu.roll(x, half, 1)
    return x * cos4 + jnp.where(low4, from_hi, from_lo) * sin4


def _inproj_kernel(pos0, pos_stride, with_aux, x_ref, g_ref, w_ref, inv_ref, *out_refs):
    if with_aux:
        k_ref, v_ref, glu_ref, q_ref, kp_ref, vt_ref, kmean_ref = out_refs
    else:
        k_ref, v_ref, glu_ref, q_ref = out_refs
    tm = x_ref.shape[0]
    xn = _rmsnorm(x_ref[...], g_ref[...]).astype(BF16)
    z = jnp.dot(xn, w_ref[...], preferred_element_type=F32)
    q = z[:, 0 * ATT_WIDTH:1 * ATT_WIDTH]
    k = z[:, 1 * ATT_WIDTH:2 * ATT_WIDTH]
    v = z[:, 2 * ATT_WIDTH:3 * ATT_WIDTH]
    ga = z[:, 3 * ATT_WIDTH:3 * ATT_WIDTH + CONV_CH]
    gb = z[:, 3 * ATT_WIDTH + CONV_CH:]

    row = lax.broadcasted_iota(jnp.int32, (tm, LANES), 0)
    d = lax.broadcasted_iota(jnp.int32, (tm, LANES), 1) % HEAD_DIM
    pos = (pos0 + (pl.program_id(0) * tm + row) * pos_stride).astype(F32)
    ang = pos * inv_ref[...]
    cos = jnp.where(d < ROT_DIM, jnp.cos(ang), 1.0)
    sin = jnp.sin(ang)
    sin = jnp.where(d < ROT_DIM // 2, -sin, jnp.where(d < ROT_DIM, sin, 0.0))
    reps = ATT_WIDTH // LANES
    cos4 = jnp.concatenate([cos] * reps, axis=1)
    sin4 = jnp.concatenate([sin] * reps, axis=1)
    low4 = lax.broadcasted_iota(jnp.int32, (tm, ATT_WIDTH), 1) % HEAD_DIM < ROT_DIM // 2

    k = _rope(k, cos4, sin4, low4)
    q = _rope(q, cos4, sin4, low4) * (HEAD_DIM ** -0.5)
    k_ref[...] = k
    v_ref[...] = v
    glu_ref[...] = ga * _sigmoid(gb)
    if with_aux:
        q_ref[...] = (q * LOG2_E).astype(BF16)
        kp_ref[...] = k.astype(BF16)
        kmean_ref[0] = jnp.mean(k, axis=0, keepdims=True)
        vt = v.T
        for h in range(N_HEADS):
            vt_ref[h, 0] = vt[h * HEAD_DIM:(h + 1) * HEAD_DIM, :].astype(BF16)
    else:
        q_ref[...] = q


def _inproj(x, g_mix, w_in_bf, inv_lane, *, pos0, pos_stride, tm, with_aux):
    t = x.shape[0]
    n_in = w_in_bf.shape[1]
    grid = (t // tm,)
    row_spec = lambda w: pl.BlockSpec((tm, w), lambda i: (i, 0))
    const = lambda shape: pl.BlockSpec(shape, lambda i: (0,) * len(shape))
    out_shape = [jax.ShapeDtypeStruct((t, ATT_WIDTH), F32), jax.ShapeDtypeStruct((t, ATT_WIDTH), F32),
                 jax.ShapeDtypeStruct((t, CONV_CH), F32)]
    out_specs = [row_spec(ATT_WIDTH), row_spec(ATT_WIDTH), row_spec(CONV_CH)]
    if with_aux:
        assert tm == MOBA_BLOCK
        nb = t // MOBA_BLOCK
        out_shape += [jax.ShapeDtypeStruct((t, ATT_WIDTH), BF16), jax.ShapeDtypeStruct((t, ATT_WIDTH), BF16),
                      jax.ShapeDtypeStruct((N_HEADS, nb, HEAD_DIM, MOBA_BLOCK), BF16),
                      jax.ShapeDtypeStruct((nb, 1, ATT_WIDTH), F32)]
        out_specs += [row_spec(ATT_WIDTH), row_spec(ATT_WIDTH),
                      pl.BlockSpec((N_HEADS, 1, HEAD_DIM, MOBA_BLOCK), lambda i: (0, i, 0, 0)),
                      pl.BlockSpec((1, 1, ATT_WIDTH), lambda i: (i, 0, 0))]
    else:
        out_shape += [jax.ShapeDtypeStruct((t, ATT_WIDTH), F32)]
        out_specs += [row_spec(ATT_WIDTH)]
    return pl.pallas_call(
        functools.partial(_inproj_kernel, pos0, pos_stride, with_aux),
        grid=grid,
        in_specs=[row_spec(D_MODEL), const((1, D_MODEL)), const((D_MODEL, n_in)), const((1, LANES))],
        out_specs=out_specs, out_shape=out_shape, compiler_params=_cparams(1),
        name="inproj_aux" if with_aux else "inproj",
    )(x, g_mix, w_in_bf, inv_lane)


def _first_max_onehot(g, idx_f, axis):
    mx = jnp.max(g, axis=axis, keepdims=True)
    first = jnp.min(jnp.where(g == mx, idx_f, jnp.inf), axis=axis, keepdims=True)
    return mx, idx_f == first


def _moba_kernel(chunk, q_ref, kp_ref, vt_ref, kmean_ref, o_ref):
    i = pl.program_id(1)
    blk = MOBA_BLOCK
    q2 = q_ref[...]
    km = kmean_ref[...].astype(BF16)
    lane = lax.broadcasted_iota(jnp.int32, (blk, LANES), 1)
    lane_f = lane.astype(F32)
    key_row = lax.broadcasted_iota(jnp.int32, (blk, blk), 0)
    qry_col = lax.broadcasted_iota(jnp.int32, (blk, blk), 1)

    qzs, biases = [], []
    for hh in range(2):
        qz = jnp.where(lane // HEAD_DIM == hh, q2, jnp.zeros_like(q2))
        gate = _dot_nt(qz, km)
        g = jnp.where(lane < i, gate, -jnp.inf)
        sel = jnp.zeros((blk, LANES), jnp.bool_)
        for _ in range(MOBA_TOPK):
            mx, hit = _first_max_onehot(g, lane_f, 1)
            sel = sel | (hit & (mx > -jnp.inf))
            g = jnp.where(hit, -jnp.inf, g)
        qzs.append(qz)
        biases.append(jnp.where(sel, 0.0, MASKED))

    k_own = kp_ref[pl.ds(pl.multiple_of(i * blk, blk), blk), :]
    state = []
    for hh in range(2):
        s = _dot_nt(k_own, qzs[hh])
        s = jnp.where(key_row <= qry_col, s, MASKED)
        m = jnp.max(s, axis=0, keepdims=True)
        p = jnp.exp2(s - m)
        l = jnp.sum(p, axis=0, keepdims=True)
        acc = jnp.dot(vt_ref[hh, i], p.astype(BF16), preferred_element_type=F32)
        state += [m, l, acc]

    def flash_step(chunk, c, carry):
        base = c * chunk
        shift = lax.rem(LANES - base, LANES)
        k_augs = []
        for r in range(chunk):
            kr = kp_ref[pl.ds(pl.multiple_of((base + r) * blk, blk), blk), :]
            k_augs.append(jnp.concatenate([kr, (lane == r).astype(BF16)], axis=1))
        out = []
        for hh in range(2):
            m, l, acc = carry[3 * hh:3 * hh + 3]
            bias_c = pltpu.roll(biases[hh], shift, 1).astype(BF16)
            q_aug = jnp.concatenate([qzs[hh], bias_c], axis=1)
            ss = [_dot_nt(ka, q_aug) for ka in k_augs]
            mc = functools.reduce(jnp.maximum, [jnp.max(s, axis=0, keepdims=True) for s in ss])
            m_new = jnp.maximum(m, mc)
            alpha = jnp.exp2(m - m_new)
            l = alpha * l
            acc = alpha * acc
            for r in range(chunk):
                p = jnp.exp2(ss[r] - m_new)
                l = l + jnp.sum(p, axis=0, keepdims=True)
                acc = acc + jnp.dot(vt_ref[hh, base + r], p.astype(BF16), preferred_element_type=F32)
            out += [m_new, l, acc]
        return tuple(out)

    full, rest = i // chunk, i % chunk
    half = chunk // 2
    short_tail = (rest > 0) & (rest <= half) if half else jnp.bool_(False)
    n_long = full + ((rest > 0) & jnp.logical_not(short_tail)).astype(jnp.int32)
    state = lax.fori_loop(0, n_long, functools.partial(flash_step, chunk), tuple(state))
    if half:
        first_half = 2 * full
        state = lax.fori_loop(first_half, first_half + short_tail.astype(jnp.int32),
                              functools.partial(flash_step, half), state)
    o_t = jnp.concatenate([state[2] / state[1], state[5] / state[4]], axis=0)
    o_ref[...] = o_t.T.astype(o_ref.dtype)


def _moba_prompt(qp, kp, vt, kmean):
    t = qp.shape[0]
    nb = t // MOBA_BLOCK
    assert t % MOBA_BLOCK == 0 and nb <= LANES
    chunk = next(c for c in (8, 4, 2, 1) if nb % c == 0)
    km = jnp.pad(kmean.reshape(nb, ATT_WIDTH), ((0, LANES - nb), (0, 0)))
    return pl.pallas_call(
        functools.partial(_moba_kernel, chunk),
        grid=(N_HEADS // 2, nb),
        in_specs=[pl.BlockSpec((MOBA_BLOCK, LANES), lambda hp, i: (i, hp)),
                  pl.BlockSpec((t, LANES), lambda hp, i: (0, hp)),
                  pl.BlockSpec((2, nb, HEAD_DIM, MOBA_BLOCK), lambda hp, i: (hp, 0, 0, 0)),
                  pl.BlockSpec((LANES, LANES), lambda hp, i: (0, hp))],
        out_specs=pl.BlockSpec((MOBA_BLOCK, LANES), lambda hp, i: (i, hp)),
        out_shape=jax.ShapeDtypeStruct((t, ATT_WIDTH), BF16),
        compiler_params=_cparams(2), name="moba",
    )(qp, kp, vt, km)


def _dec_scores_kernel(bps, pt_ref, qb_ref, *refs):
    del pt_ref
    k_refs = refs[:2 * bps]
    s_ref, sel_ref, gate_s = refs[2 * bps:]
    step = pl.program_id(1)
    blk_lane = lax.broadcasted_iota(jnp.int32, (N_HEADS, LANES), 1)

    @pl.when(step == 0)
    def _():
        gate_s[...] = jnp.full_like(gate_s, -jnp.inf)

    qb = qb_ref[0]
    for bi in range(bps):
        s = jnp.concatenate([jnp.sum(k_refs[2 * bi + w][0] * qb, axis=1) for w in range(2)], axis=1)
        s_ref[0, :, bi * MOBA_BLOCK:(bi + 1) * MOBA_BLOCK] = s
        gate_s[...] = jnp.where(blk_lane == step * bps + bi, jnp.mean(s, axis=1, keepdims=True), gate_s[...])

    @pl.when(step == pl.num_programs(1) - 1)
    def _():
        g = gate_s[...]
        lane_f = blk_lane.astype(F32)
        out = jnp.full((N_HEADS, LANES), -1.0, F32)
        for r in range(MOBA_TOPK):
            mx, hit = _first_max_onehot(g, lane_f, 1)
            idx = jnp.min(jnp.where(hit, lane_f, jnp.inf), axis=1, keepdims=True)
            out = jnp.where((blk_lane == r) & (mx > -jnp.inf), idx, out)
            g = jnp.where(hit, -jnp.inf, g)
        sel_ref[0] = out


def _dec_combine_kernel(pt_ref, sel_ref, q_ref, kn_ref, vn_ref, *refs):
    del pt_ref
    s_refs = refs[:MOBA_TOPK]
    v_refs = refs[MOBA_TOPK:3 * MOBA_TOPK]
    o_ref = refs[3 * MOBA_TOPK]
    b, h = pl.program_id(0), pl.program_id(1)
    valid = [sel_ref[b, MOBA_TOPK * h + r] >= 0 for r in range(MOBA_TOPK)]
    s_own = jnp.sum(kn_ref[0] * q_ref[0], axis=1, keepdims=True)
    scores = [jnp.where(valid[r], s_refs[r][0, pl.ds(h, 1), :], -jnp.inf) for r in range(MOBA_TOPK)]
    big = functools.reduce(jnp.maximum, [jnp.max(sc, axis=1, keepdims=True) for sc in scores] + [s_own])
    w_own = jnp.exp(s_own - big)
    den = w_own
    num = w_own * vn_ref[0]
    for r in range(MOBA_TOPK):
        p = jnp.exp(scores[r] - big)
        den = den + jnp.sum(p, axis=1, keepdims=True)
        pv = sum(v_refs[2 * r + w][0] * p[:, None, w * PAGE_SIZE:(w + 1) * PAGE_SIZE] for w in range(2))
        num = num + jnp.sum(pv, axis=2)
    o_ref[0] = num / den


def _moba_decode(q_s, k_new, v_new, cache_k, cache_v, page_table):
    b = q_s.shape[0]
    n_pages = page_table.shape[1]
    pages_per_blk = MOBA_BLOCK // PAGE_SIZE
    assert pages_per_blk == 2 and n_pages % pages_per_blk == 0
    nbp = n_pages // pages_per_blk
    assert nbp <= LANES
    bps = next(c for c in (8, 4, 2, 1) if nbp % c == 0)
    past = nbp * MOBA_BLOCK
    ck = jnp.transpose(cache_k, (0, 2, 3, 1))
    cv = jnp.transpose(cache_v, (0, 2, 3, 1))
    q_lanes = jnp.broadcast_to(q_s.reshape(b, N_HEADS, HEAD_DIM)[..., None], (b, N_HEADS, HEAD_DIM, LANES))
    page = lambda j: pl.BlockSpec((1, N_HEADS, HEAD_DIM, PAGE_SIZE),
                                  lambda s, n, pt: (pt[s, pages_per_blk * bps * n + j], 0, 0, 0))
    pages = [page(j) for j in range(pages_per_blk * bps)]
    scores, sel = pl.pallas_call(
        functools.partial(_dec_scores_kernel, bps),
        grid_spec=pltpu.PrefetchScalarGridSpec(
            num_scalar_prefetch=1, grid=(b, nbp // bps),
            in_specs=[pl.BlockSpec((1, N_HEADS, HEAD_DIM, LANES), lambda s, n, pt: (s, 0, 0, 0))] + pages,
            out_specs=[pl.BlockSpec((1, N_HEADS, bps * MOBA_BLOCK), lambda s, n, pt: (s, 0, n)),
                       pl.BlockSpec((1, N_HEADS, LANES), lambda s, n, pt: (s, 0, 0))],
            scratch_shapes=[pltpu.VMEM((N_HEADS, LANES), F32)]),
        out_shape=[jax.ShapeDtypeStruct((b, N_HEADS, past), F32), jax.ShapeDtypeStruct((b, N_HEADS, LANES), F32)],
        compiler_params=_cparams(2), name="dec_scores",
    )(page_table, q_lanes, *([ck] * len(pages)))

    sel_i = sel[:, :, :MOBA_TOPK].astype(jnp.int32).reshape(b, N_HEADS * MOBA_TOPK)
    rows = lambda a: a.reshape(b * N_HEADS, 1, HEAD_DIM)
    pick = lambda s, h, sl, r: jnp.maximum(sl[s, MOBA_TOPK * h + r], 0)
    row_spec = pl.BlockSpec((1, 1, HEAD_DIM), lambda s, h, pt, sl: (s * N_HEADS + h, 0, 0))
    score_spec = lambda r: pl.BlockSpec((1, N_HEADS, MOBA_BLOCK), lambda s, h, pt, sl: (s, 0, pick(s, h, sl, r)))
    value_spec = lambda r, w: pl.BlockSpec(
        (1, 1, HEAD_DIM, PAGE_SIZE), lambda s, h, pt, sl: (pt[s, pages_per_blk * pick(s, h, sl, r) + w], h, 0, 0))
    value_specs = [value_spec(r, w) for r in range(MOBA_TOPK) for w in range(pages_per_blk)]
    out = pl.pallas_call(
        _dec_combine_kernel,
        grid_spec=pltpu.PrefetchScalarGridSpec(
            num_scalar_prefetch=2, grid=(b, N_HEADS),
            in_specs=[row_spec, row_spec, row_spec] + [score_spec(r) for r in range(MOBA_TOPK)] + value_specs,
            out_specs=row_spec),
        out_shape=jax.ShapeDtypeStruct((b * N_HEADS, 1, HEAD_DIM), F32),
        compiler_params=_cparams(2), name="dec_combine",
    )(page_table, sel_i, rows(q_s), rows(k_new), rows(v_new),
      *([scores] * MOBA_TOPK), *([cv] * len(value_specs)))
    return out.reshape(b, ATT_WIDTH)


def _post_conv(y, attn_bf, x, b_dw, ln_g, ln_b, w_out):
    y = y + b_dw
    mu = jnp.mean(y, axis=-1, keepdims=True)
    yc = y - mu
    var = jnp.mean(yc * yc, axis=-1, keepdims=True)
    y = yc * lax.rsqrt(var + NORM_EPS) * ln_g + ln_b
    y = y * _sigmoid(y)
    cat = jnp.concatenate([attn_bf, y.astype(BF16)], axis=1)
    return x + jnp.dot(cat, w_out, preferred_element_type=F32)


def _mix_kernel(glu_ref, halo_ref, attn_ref, x_ref, wdw_ref, bdw_ref, lng_ref, lnb_ref, wout_ref, h_ref, xin):
    tm = glu_ref.shape[0]
    first = pl.program_id(0) == 0
    xin[0:CONV_HALO, :] = jnp.where(first, 0.0, halo_ref[...])
    xin[CONV_HALO:CONV_HALO + tm, :] = glu_ref[...]
    xin[CONV_HALO + tm:, :] = jnp.zeros((SUBLANES, CONV_CH), F32)
    off = CONV_HALO - (CONV_K - 1)
    y = jnp.zeros((tm, CONV_CH), F32)
    for b in range(SUBLANES):
        z = None
        for a in range((CONV_HALO + SUBLANES) // SUBLANES):
            k = SUBLANES * a + b - off
            if 0 <= k < CONV_K:
                term = wdw_ref[k:k + 1, :] * xin[SUBLANES * a:SUBLANES * a + tm + SUBLANES, :]
                z = term if z is None else z + term
        if z is not None:
            y = y + z[b:b + tm]
    h_ref[...] = _post_conv(y, attn_ref[...], x_ref[...], bdw_ref[...], lng_ref[...], lnb_ref[...], wout_ref[...])


def _mix_prompt(glu, attn_bf, x, w_dw, b_dw, ln_g, ln_b, w_out_bf, *, tm):
    t = glu.shape[0]
    assert tm % CONV_HALO == 0 and t % tm == 0
    ratio = tm // CONV_HALO
    const = lambda shape: pl.BlockSpec(shape, lambda i: (0,) * len(shape))
    return pl.pallas_call(
        _mix_kernel,
        grid=(t // tm,),
        in_specs=[pl.BlockSpec((tm, CONV_CH), lambda i: (i, 0)),
                  pl.BlockSpec((CONV_HALO, CONV_CH), lambda i: (jnp.maximum(i * ratio - 1, 0), 0)),
                  pl.BlockSpec((tm, ATT_WIDTH), lambda i: (i, 0)),
                  pl.BlockSpec((tm, D_MODEL), lambda i: (i, 0)),
                  const((CONV_HALO, CONV_CH)), const((1, CONV_CH)), const((1, CONV_CH)), const((1, CONV_CH)),
                  const((D_MODEL, D_MODEL))],
        out_specs=pl.BlockSpec((tm, D_MODEL), lambda i: (i, 0)),
        out_shape=jax.ShapeDtypeStruct((t, D_MODEL), F32),
        scratch_shapes=[pltpu.VMEM((tm + CONV_HALO + SUBLANES, CONV_CH), F32)],
        compiler_params=_cparams(1), name="mix",
    )(glu, glu, attn_bf, x, w_dw, b_dw, ln_g, ln_b, w_out_bf)


def _mix_dec_kernel(xin_ref, attn_ref, x_ref, wdw_ref, bdw_ref, lng_ref, lnb_ref, wout_ref, h_ref):
    y = jnp.sum(xin_ref[...] * wdw_ref[...][None], axis=1)
    h_ref[...] = _post_conv(y, attn_ref[...].astype(BF16), x_ref[...], bdw_ref[...], lng_ref[...],
                            lnb_ref[...], wout_ref[...])


def _mix_decode(xin, attn, x, w_dw, b_dw, ln_g, ln_b, w_out_bf):
    b = x.shape[0]
    full = lambda a: pl.BlockSpec(a.shape, lambda i: (0,) * a.ndim)
    args = (xin, attn, x, w_dw, b_dw, ln_g, ln_b, w_out_bf)
    return pl.pallas_call(
        _mix_dec_kernel, grid=(1,), in_specs=[full(a) for a in args],
        out_specs=pl.BlockSpec((b, D_MODEL), lambda i: (0, 0)),
        out_shape=jax.ShapeDtypeStruct((b, D_MODEL), F32),
        compiler_params=_cparams(1), name="mix_dec",
    )(*args)


def _hyperbola_cells():
    return [(i, j) for i in range(PEER_TOPK) for j in range(PEER_TOPK) if (i + 1) * (j + 1) <= PEER_TOPK]


def _route_kernel(h_ref, g_ref, wq_ref, keys_ref, xn_ref, nc1_ref, r2_ref, e1_ref, e2_ref,
                  sc_s, rank_s, top_s, cnt_s, z_s):
    tm = h_ref.shape[0]
    nk = PEER_NKEYS
    xn = _rmsnorm(h_ref[...], g_ref[...]).astype(BF16)
    xn_ref[...] = xn
    qb = jnp.dot(xn, wq_ref[...], preferred_element_type=F32).astype(BF16)
    for hc in range(2 * PEER_HEADS):
        sc_s[hc] = _dot_nt(keys_ref[hc], qb[:, hc * nk:(hc + 1) * nk])

    key_f = lax.broadcasted_iota(jnp.int32, (nk, tm), 0).astype(F32)
    sub = lax.broadcasted_iota(jnp.int32, (PEER_HEADS, tm), 0)
    cells = _hyperbola_cells()

    def rank_rounds(hcs, exact_ties):
        xs = [sc_s[hc] for hc in hcs]
        ranks = [jnp.full((nk, tm), float(PEER_TOPK), F32) for _ in hcs]
        for r in range(PEER_TOPK):
            for n, hc in enumerate(hcs):
                if exact_ties:
                    mx, hit = _first_max_onehot(xs[n], key_f, 0)
                else:
                    mx = jnp.max(xs[n], axis=0, keepdims=True)
                    hit = xs[n] == mx
                ranks[n] = jnp.where(hit, float(r), ranks[n])
                xs[n] = jnp.where(hit, -jnp.inf, xs[n])
                top_s[hc, r] = jnp.broadcast_to(mx, (8, tm))
        for n, hc in enumerate(hcs):
            rank_s[hc] = ranks[n]
        return ranks

    def rank_body(h, carry):
        hcs = (2 * h, 2 * h + 1)
        ranks = rank_rounds(hcs, exact_ties=False)
        ranked = sum(jnp.sum(jnp.where(rk < float(PEER_TOPK), 1.0, 0.0), axis=0, keepdims=True) for rk in ranks)
        tied = jnp.max(jnp.abs(ranked - float(len(hcs) * PEER_TOPK))) > 0.0

        @pl.when(tied)
        def _():
            rank_rounds(hcs, exact_ties=True)

        return carry

    lax.fori_loop(0, PEER_HEADS, rank_body, 0)

    def heads_on_sublanes(side, r):
        out = top_s[side, r]
        for h in range(1, PEER_HEADS):
            out = jnp.where(sub == h, top_s[2 * h + side, r], out)
        return out

    a = [heads_on_sublanes(0, r) for r in range(PEER_TOPK)]
    bb = [heads_on_sublanes(1, r) for r in range(PEER_TOPK)]
    cur0 = tuple(a[i] + bb[j] for (i, j) in cells)
    best = cur0[0]

    def pick_body(_, carry):
        cur, taken = carry
        mx = functools.reduce(jnp.maximum, cur)
        first = functools.reduce(jnp.minimum, [jnp.where(cur[ci] == mx, float(ci), jnp.inf)
                                               for ci in range(len(cells))])
        hits = [first == float(ci) for ci in range(len(cells))]
        return (tuple(jnp.where(hit, -jnp.inf, c) for hit, c in zip(hits, cur)),
                tuple(jnp.where(hit, 1.0, t) for hit, t in zip(hits, taken)))

    state = (cur0, tuple(jnp.zeros_like(best) for _ in cells))
    for r in range(PEER_TOPK):
        state = pick_body(r, state)
    taken = state[1]
    z = jnp.zeros_like(best)
    counts = [jnp.zeros_like(best) for _ in range(PEER_TOPK)]
    for ci, (i, j) in enumerate(cells):
        counts[i] = counts[i] + taken[ci]
        z = z + taken[ci] * jnp.exp(a[i] + bb[j] - best)
    for h in range(PEER_HEADS):
        z_s[h] = jnp.broadcast_to(z[h:h + 1], (8, tm))
        for r in range(PEER_TOPK):
            cnt_s[h, r] = jnp.broadcast_to(counts[r][h:h + 1], (8, tm))

    def gate_body(h, carry):
        r1 = rank_s[2 * h]
        nc = jnp.zeros((nk, tm), F32)
        for r in range(PEER_TOPK):
            nc = jnp.where(r1 == float(r), cnt_s[h, r, 0:1, :], nc)
        nc1_ref[h] = nc
        r2_ref[h] = rank_s[2 * h + 1].astype(r2_ref.dtype)
        a0 = top_s[2 * h, 0, 0:1, :]
        b0 = top_s[2 * h + 1, 0, 0:1, :]
        e1_ref[h] = jnp.exp(sc_s[2 * h] - a0) / z_s[h, 0:1, :]
        e2_ref[h] = jnp.exp(sc_s[2 * h + 1] - b0).astype(e2_ref.dtype)
        return carry

    lax.fori_loop(0, PEER_HEADS, gate_body, 0)


def _peer_route(h1, g_ffn, wq_bf, keys_bf, *, tm):
    t = h1.shape[0]
    assert t % tm == 0 and tm % LANES == 0
    const = lambda shape: pl.BlockSpec(shape, lambda i: (0,) * len(shape))
    gate_shape = lambda dt: jax.ShapeDtypeStruct((PEER_HEADS, PEER_NKEYS, t), dt)
    gate_spec = pl.BlockSpec((PEER_HEADS, PEER_NKEYS, tm), lambda i: (0, 0, i))
    return pl.pallas_call(
        _route_kernel,
        grid=(t // tm,),
        in_specs=[pl.BlockSpec((tm, D_MODEL), lambda i: (i, 0)), const((1, D_MODEL)),
                  const(wq_bf.shape), const(keys_bf.shape)],
        out_specs=[pl.BlockSpec((tm, D_MODEL), lambda i: (i, 0))] + [gate_spec] * 4,
        out_shape=[jax.ShapeDtypeStruct((t, D_MODEL), BF16),
                   gate_shape(F32), gate_shape(BF16), gate_shape(F32), gate_shape(BF16)],
        scratch_shapes=[pltpu.VMEM((2 * PEER_HEADS, PEER_NKEYS, tm), F32),
                        pltpu.VMEM((2 * PEER_HEADS, PEER_NKEYS, tm), F32),
                        pltpu.VMEM((2 * PEER_HEADS, PEER_TOPK, 8, tm), F32),
                        pltpu.VMEM((PEER_HEADS, PEER_TOPK, 8, tm), F32),
                        pltpu.VMEM((PEER_HEADS, 8, tm), F32)],
        compiler_params=_cparams(1), name="route",
    )(h1, g_ffn, wq_bf, keys_bf)


def _experts_kernel(xn_ref, u_ref, vt_ref, nc1_ref, r2_ref, e1_ref, e2_ref, h_ref, o_ref, acc, w_s):
    c = pl.program_id(1)
    last = pl.num_programs(1) - 1

    @pl.when(c == 0)
    def _():
        acc[...] = jnp.zeros_like(acc)
        w_s[...] = jnp.zeros_like(w_s)

    acc[...] += jnp.dot(vt_ref[...], w_s[...], preferred_element_type=F32)

    blk = jnp.minimum(c, last - 1)
    pre = _dot_nt(u_ref[...], xn_ref[...])
    act = (0.5 * pre * (1.0 + lax.erf(pre * (2.0 ** -0.5)))).astype(BF16)
    n_i1 = EXPERT_BLOCK // PEER_NKEYS
    zero = jnp.zeros((PEER_NKEYS, act.shape[1]), BF16)
    gates = [zero] * n_i1
    for h in range(PEER_HEADS):
        r2 = r2_ref[h]
        e2 = e2_ref[h]
        for u in range(n_i1):
            i1 = blk * n_i1 + u
            nc = nc1_ref[h, pl.ds(i1, 1), :].astype(BF16)
            e1 = e1_ref[h, pl.ds(i1, 1), :].astype(BF16)
            gates[u] = gates[u] + jnp.where(r2 < nc, e2, zero) * e1
    w_s[...] = jnp.concatenate(gates, axis=0) * act

    @pl.when(c == last)
    def _():
        o_ref[...] = h_ref[...] + acc[...].T


def _peer_experts(xn, u_bf, vt_bf, nc1, r2, e1, e2, h1, *, tm):
    t = xn.shape[0]
    n_exp = u_bf.shape[0]
    assert t % tm == 0 and n_exp % EXPERT_BLOCK == 0
    n_blk = n_exp // EXPERT_BLOCK
    once = dict(pipeline_mode=pl.Buffered(1))
    gate_spec = pl.BlockSpec((PEER_HEADS, PEER_NKEYS, tm), lambda i, c: (0, 0, i), **once)
    tok_spec = pl.BlockSpec((tm, D_MODEL), lambda i, c: (i, 0))
    return pl.pallas_call(
        _experts_kernel,
        grid=(t // tm, n_blk + 1),
        in_specs=[tok_spec,
                  pl.BlockSpec((EXPERT_BLOCK, D_MODEL), lambda i, c: (jnp.minimum(c, n_blk - 1), 0)),
                  pl.BlockSpec((D_MODEL, EXPERT_BLOCK), lambda i, c: (0, jnp.maximum(c - 1, 0))),
                  gate_spec, gate_spec, gate_spec, gate_spec,
                  pl.BlockSpec((tm, D_MODEL), lambda i, c: (i, 0), **once)],
        out_specs=tok_spec,
        out_shape=jax.ShapeDtypeStruct((t, D_MODEL), F32),
        scratch_shapes=[pltpu.VMEM((D_MODEL, tm), F32), pltpu.VMEM((EXPERT_BLOCK, tm), BF16)],
        compiler_params=_cparams(2), name="experts",
    )(xn, u_bf, vt_bf, nc1, r2, e1, e2, h1)


def _ple_kernel(h_ref, p_ref, gple_ref, wg_ref, wp_ref, gfin_ref, y_ref):
    h = h_ref[...]
    gate = _sigmoid(jnp.dot(_rmsnorm(h, gple_ref[...]).astype(BF16), wg_ref[...], preferred_element_type=F32))
    proj = jnp.dot(p_ref[...].astype(BF16), wp_ref[...], preferred_element_type=F32)
    y_ref[...] = _rmsnorm(h + gate * proj, gfin_ref[...])


def _ple_final(h2, p_emb, g_ple, wg_bf, wp_bf, g_final, *, tm):
    t = h2.shape[0]
    const = lambda shape: pl.BlockSpec(shape, lambda i: (0,) * len(shape))
    return pl.pallas_call(
        _ple_kernel,
        grid=(t // tm,),
        in_specs=[pl.BlockSpec((tm, D_MODEL), lambda i: (i, 0)), pl.BlockSpec((tm, PLE_DIM), lambda i: (i, 0)),
                  const((1, D_MODEL)), const((D_MODEL, D_MODEL)), const((PLE_DIM, D_MODEL)), const((1, D_MODEL))],
        out_specs=pl.BlockSpec((tm, D_MODEL), lambda i: (i, 0)),
        out_shape=jax.ShapeDtypeStruct((t, D_MODEL), F32),
        compiler_params=_cparams(1), name="ple",
    )(h2, p_emb, g_ple, wg_bf, wp_bf, g_final)


def _pick_tile(t, want):
    tm = min(want, t)
    assert t % tm == 0
    return tm


def _ffn(h1, p_emb, lw, g_final):
    t = h1.shape[0]
    xn, nc1, r2, e1, e2 = _peer_route(h1, lw["g_ffn"], lw["wq"], lw["keys"], tm=_pick_tile(t, 256))
    h2 = _peer_experts(xn, lw["u"], lw["vt"], nc1, r2, e1, e2, h1, tm=_pick_tile(t, 1024))
    return _ple_final(h2, p_emb, lw["g_ple"], lw["w_ple_gate"], lw["w_ple_proj"], g_final, tm=_pick_tile(t, 256))


def kernel(x_prompt, x_sample, cache_k, cache_v, cache_conv, page_table, p_prompt, p_sample,
           g_mix, w_in, w_dw, b_dw, ln_g, ln_b, w_out, g_ffn, peer_wq, peer_keys, peer_u, peer_v,
           g_ple, w_ple_gate, w_ple_proj, g_final):
    depth = g_mix.shape[0]
    batch, seq, _ = x_prompt.shape
    dec_batch, dec_seq, _ = x_sample.shape
    assert depth == 1 and batch == 1 and dec_seq == 1
    past_len = page_table.shape[1] * PAGE_SIZE
    assert seq % MOBA_BLOCK == 0 and past_len % MOBA_BLOCK == 0 and seq >= CONV_K - 1
    row = lambda a: a.reshape(1, -1)
    lane = jnp.arange(LANES) % (ROT_DIM // 2)
    inv_lane = (ROPE_THETA ** (-lane.astype(F32) * 2.0 / ROT_DIM)).reshape(1, LANES)
    l = 0
    lw = dict(
        g_ffn=row(g_ffn[l]), wq=peer_wq[l].astype(BF16),
        keys=peer_keys[l].reshape(2 * PEER_HEADS, PEER_NKEYS, -1).astype(BF16),
        u=peer_u[l].astype(BF16), vt=peer_v[l].T.astype(BF16),
        g_ple=row(g_ple[l]), w_ple_gate=w_ple_gate[l].astype(BF16), w_ple_proj=w_ple_proj[l].astype(BF16))
    w_in_bf = w_in[l].astype(BF16)
    w_out_bf = w_out[l].astype(BF16)
    w_dw_pad = jnp.pad(w_dw[l], ((0, CONV_HALO - CONV_K), (0, 0)))
    conv_args = (w_dw_pad, row(b_dw[l]), row(ln_g[l]), row(ln_b[l]), w_out_bf)
    g_fin = row(g_final)

    xp = x_prompt.reshape(seq, D_MODEL)
    k_p, v_p, glu_p, qp, kp, vt, kmean = _inproj(xp, row(g_mix[l]), w_in_bf, inv_lane, pos0=0, pos_stride=1,
                                                 tm=MOBA_BLOCK, with_aux=True)
    attn_p = _moba_prompt(qp, kp, vt, kmean)
    h1_p = _mix_prompt(glu_p, attn_p, xp, *conv_args, tm=_pick_tile(seq, 256))
    y_p = _ffn(h1_p, p_prompt[l].reshape(seq, PLE_DIM), lw, g_fin)

    xs = x_sample.reshape(dec_batch, D_MODEL)
    k_s, v_s, glu_s, q_s = _inproj(xs, row(g_mix[l]), w_in_bf, inv_lane, pos0=past_len, pos_stride=0,
                                   tm=dec_batch, with_aux=False)
    attn_s = _moba_decode(q_s, k_s, v_s, cache_k[l], cache_v[l], page_table)
    xin_s = jnp.concatenate([cache_conv[l], glu_s[:, None, :]], axis=1)
    xin_pad = jnp.pad(xin_s, ((0, 0), (0, CONV_HALO - CONV_K), (0, 0)))
    h1_s = _mix_decode(xin_pad, attn_s, xs, *conv_args)
    pad = (-dec_batch) % LANES
    h1_s = jnp.pad(h1_s, ((0, pad), (0, 0)))
    p_s = jnp.pad(p_sample[l].reshape(dec_batch, PLE_DIM), ((0, pad), (0, 0)))
    y_s = _ffn(h1_s, p_s, lw, g_fin)[:dec_batch]

    conv_p = glu_p[seq - (CONV_K - 1):].reshape(1, 1, CONV_K - 1, CONV_CH)
    conv_s = xin_s[:, 1:].reshape(1, dec_batch, CONV_K - 1, CONV_CH)
    return (y_p.reshape(1, seq, D_MODEL), y_s.reshape(dec_batch, 1, D_MODEL),
            k_p.reshape(1, 1, seq, N_HEADS, HEAD_DIM), v_p.reshape(1, 1, seq, N_HEADS, HEAD_DIM), conv_p,
            k_s.reshape(1, dec_batch, 1, N_HEADS, HEAD_DIM), v_s.reshape(1, dec_batch, 1, N_HEADS, HEAD_DIM),
            conv_s)
```

```python
import functools

import jax
import jax.numpy as jnp
from jax import lax
from jax.experimental import pallas as pl
from jax.experimental.pallas import tpu as pltpu

F32 = jnp.float32
BF16 = jnp.bfloat16

D_MODEL = 1024
N_HEADS = 8
HEAD_DIM = 64
ATT_WIDTH = N_HEADS * HEAD_DIM
CONV_CH = D_MODEL - ATT_WIDTH
ROT_DIM = HEAD_DIM // 4
ROPE_THETA = 500000.0
MOBA_BLOCK = 256
MOBA_TOPK = 3
CONV_K = 31
PAGE_SIZE = 128
PEER_HEADS = 8
PEER_NKEYS = 128
PEER_TOPK = 16
PLE_DIM = 256
NORM_EPS = 1e-6

LANES = 128
SUBLANES = 8
CONV_HALO = 32
MASKED = -1e30
EXPERT_BLOCK = 8 * PEER_NKEYS
VMEM_LIMIT = 56 * 1024 * 1024
LOG2_E = 1.4426950408889634


def _cparams(n_axes):
    return pltpu.CompilerParams(dimension_semantics=("arbitrary",) * n_axes, vmem_limit_bytes=VMEM_LIMIT)


def _rmsnorm(x, g):
    return x * lax.rsqrt(jnp.mean(x * x, axis=-1, keepdims=True) + NORM_EPS) * g


def _sigmoid(x):
    return 1.0 / (1.0 + jnp.exp(-x))


def _dot_nt(a, b):
    return lax.dot_general(a, b, (((1,), (1,)), ((), ())), preferred_element_type=F32)


def _rope(x, cos4, sin4, low4):
    half = ROT_DIM // 2
    from_hi = pltpu.roll(x, ATT_WIDTH - half, 1)
    from_lo = pltpu.roll(x, half, 1)
    return x * cos4 + jnp.where(low4, from_hi, from_lo) * sin4


def _inproj_kernel(pos0, pos_stride, with_aux, x_ref, g_ref, w_ref, inv_ref, *out_refs):
    if with_aux:
        k_ref, v_ref, glu_ref, q_ref, kp_ref, vt_ref, kmean_ref = out_refs
    else:
        k_ref, v_ref, glu_ref, q_ref = out_refs
    tm = x_ref.shape[0]
    xn = _rmsnorm(x_ref[...], g_ref[...]).astype(BF16)
    z = jnp.dot(xn, w_ref[...], preferred_element_type=F32)
    q = z[:, 0 * ATT_WIDTH:1 * ATT_WIDTH]
    k = z[:, 1 * ATT_WIDTH:2 * ATT_WIDTH]
    v = z[:, 2 * ATT_WIDTH:3 * ATT_WIDTH]
    ga = z[:, 3 * ATT_WIDTH:3 * ATT_WIDTH + CONV_CH]
    gb = z[:, 3 * ATT_WIDTH + CONV_CH:]

    row = lax.broadcasted_iota(jnp.int32, (tm, LANES), 0)
    d = lax.broadcasted_iota(jnp.int32, (tm, LANES), 1) % HEAD_DIM
    pos = (pos0 + (pl.program_id(0) * tm + row) * pos_stride).astype(F32)
    ang = pos * inv_ref[...]
    cos = jnp.where(d < ROT_DIM, jnp.cos(ang), 1.0)
    sin = jnp.sin(ang)
    sin = jnp.where(d < ROT_DIM // 2, -sin, jnp.where(d < ROT_DIM, sin, 0.0))
    reps = ATT_WIDTH // LANES
    cos4 = jnp.concatenate([cos] * reps, axis=1)
    sin4 = jnp.concatenate([sin] * reps, axis=1)
    low4 = lax.broadcasted_iota(jnp.int32, (tm, ATT_WIDTH), 1) % HEAD_DIM < ROT_DIM // 2

    k = _rope(k, cos4, sin4, low4)
    q = _rope(q, cos4, sin4, low4) * (HEAD_DIM ** -0.5)
    k_ref[...] = k
    v_ref[...] = v
    glu_ref[...] = ga * _sigmoid(gb)
    if with_aux:
        q_ref[...] = (q * LOG2_E).astype(BF16)
        kp_ref[...] = k.astype(BF16)
        kmean_ref[0] = jnp.mean(k, axis=0, keepdims=True)
        vt = v.T
        for h in range(N_HEADS):
            vt_ref[h, 0] = vt[h * HEAD_DIM:(h + 1) * HEAD_DIM, :].astype(BF16)
    else:
        q_ref[...] = q


def _inproj(x, g_mix, w_in_bf, inv_lane, *, pos0, pos_stride, tm, with_aux):
    t = x.shape[0]
    n_in = w_in_bf.shape[1]
    grid = (t // tm,)
    row_spec = lambda w: pl.BlockSpec((tm, w), lambda i: (i, 0))
    const = lambda shape: pl.BlockSpec(shape, lambda i: (0,) * len(shape))
    out_shape = [jax.ShapeDtypeStruct((t, ATT_WIDTH), F32), jax.ShapeDtypeStruct((t, ATT_WIDTH), F32),
                 jax.ShapeDtypeStruct((t, CONV_CH), F32)]
    out_specs = [row_spec(ATT_WIDTH), row_spec(ATT_WIDTH), row_spec(CONV_CH)]
    if with_aux:
        assert tm == MOBA_BLOCK
        nb = t // MOBA_BLOCK
        out_shape += [jax.ShapeDtypeStruct((t, ATT_WIDTH), BF16), jax.ShapeDtypeStruct((t, ATT_WIDTH), BF16),
                      jax.ShapeDtypeStruct((N_HEADS, nb, HEAD_DIM, MOBA_BLOCK), BF16),
                      jax.ShapeDtypeStruct((nb, 1, ATT_WIDTH), F32)]
        out_specs += [row_spec(ATT_WIDTH), row_spec(ATT_WIDTH),
                      pl.BlockSpec((N_HEADS, 1, HEAD_DIM, MOBA_BLOCK), lambda i: (0, i, 0, 0)),
                      pl.BlockSpec((1, 1, ATT_WIDTH), lambda i: (i, 0, 0))]
    else:
        out_shape += [jax.ShapeDtypeStruct((t, ATT_WIDTH), F32)]
        out_specs += [row_spec(ATT_WIDTH)]
    return pl.pallas_call(
        functools.partial(_inproj_kernel, pos0, pos_stride, with_aux),
        grid=grid,
        in_specs=[row_spec(D_MODEL), const((1, D_MODEL)), const((D_MODEL, n_in)), const((1, LANES))],
        out_specs=out_specs, out_shape=out_shape, compiler_params=_cparams(1),
        name="inproj_aux" if with_aux else "inproj",
    )(x, g_mix, w_in_bf, inv_lane)


def _first_max_onehot(g, idx_f, axis):
    mx = jnp.max(g, axis=axis, keepdims=True)
    first = jnp.min(jnp.where(g == mx, idx_f, jnp.inf), axis=axis, keepdims=True)
    return mx, idx_f == first


def _moba_kernel(chunk, q_ref, kp_ref, vt_ref, kmean_ref, o_ref):
    i = pl.program_id(1)
    blk = MOBA_BLOCK
    q2 = q_ref[...]
    km = kmean_ref[...].astype(BF16)
    lane = lax.broadcasted_iota(jnp.int32, (blk, LANES), 1)
    lane_f = lane.astype(F32)
    key_row = lax.broadcasted_iota(jnp.int32, (blk, blk), 0)
    qry_col = lax.broadcasted_iota(jnp.int32, (blk, blk), 1)

    qzs, biases = [], []
    for hh in range(2):
        qz = jnp.where(lane // HEAD_DIM == hh, q2, jnp.zeros_like(q2))
        gate = _dot_nt(qz, km)
        g = jnp.where(lane < i, gate, -jnp.inf)
        sel = jnp.zeros((blk, LANES), jnp.bool_)
        for _ in range(MOBA_TOPK):
            mx, hit = _first_max_onehot(g, lane_f, 1)
            sel = sel | (hit & (mx > -jnp.inf))
            g = jnp.where(hit, -jnp.inf, g)
        qzs.append(qz)
        biases.append(jnp.where(sel, 0.0, MASKED))

    k_own = kp_ref[pl.ds(pl.multiple_of(i * blk, blk), blk), :]
    state = []
    for hh in range(2):
        s = _dot_nt(k_own, qzs[hh])
        s = jnp.where(key_row <= qry_col, s, MASKED)
        m = jnp.max(s, axis=0, keepdims=True)
        p = jnp.exp2(s - m)
        l = jnp.sum(p, axis=0, keepdims=True)
        acc = jnp.dot(vt_ref[hh, i], p.astype(BF16), preferred_element_type=F32)
        state += [m, l, acc]

    def flash_step(chunk, c, carry):
        base = c * chunk
        shift = lax.rem(LANES - base, LANES)
        k_augs = []
        for r in range(chunk):
            kr = kp_ref[pl.ds(pl.multiple_of((base + r) * blk, blk), blk), :]
            k_augs.append(jnp.concatenate([kr, (lane == r).astype(BF16)], axis=1))
        out = []
        for hh in range(2):
            m, l, acc = carry[3 * hh:3 * hh + 3]
            bias_c = pltpu.roll(biases[hh], shift, 1).astype(BF16)
            q_aug = jnp.concatenate([qzs[hh], bias_c], axis=1)
            ss = [_dot_nt(ka, q_aug) for ka in k_augs]
            mc = functools.reduce(jnp.maximum, [jnp.max(s, axis=0, keepdims=True) for s in ss])
            m_new = jnp.maximum(m, mc)
            alpha = jnp.exp2(m - m_new)
            l = alpha * l
            acc = alpha * acc
            for r in range(chunk):
                p = jnp.exp2(ss[r] - m_new)
                l = l + jnp.sum(p, axis=0, keepdims=True)
                acc = acc + jnp.dot(vt_ref[hh, base + r], p.astype(BF16), preferred_element_type=F32)
            out += [m_new, l, acc]
        return tuple(out)

    full, rest = i // chunk, i % chunk
    half = chunk // 2
    short_tail = (rest > 0) & (rest <= half) if half else jnp.bool_(False)
    n_long = full + ((rest > 0) & jnp.logical_not(short_tail)).astype(jnp.int32)
    state = lax.fori_loop(0, n_long, functools.partial(flash_step, chunk), tuple(state))
    if half:
        first_half = 2 * full
        state = lax.fori_loop(first_half, first_half + short_tail.astype(jnp.int32),
                              functools.partial(flash_step, half), state)
    o_t = jnp.concatenate([state[2] / state[1], state[5] / state[4]], axis=0)
    o_ref[...] = o_t.T.astype(o_ref.dtype)


def _moba_prompt(qp, kp, vt, kmean):
    t = qp.shape[0]
    nb = t // MOBA_BLOCK
    assert t % MOBA_BLOCK == 0 and nb <= LANES
    chunk = next(c for c in (8, 4, 2, 1) if nb % c == 0)
    km = jnp.pad(kmean.reshape(nb, ATT_WIDTH), ((0, LANES - nb), (0, 0)))
    return pl.pallas_call(
        functools.partial(_moba_kernel, chunk),
        grid=(N_HEADS // 2, nb),
        in_specs=[pl.BlockSpec((MOBA_BLOCK, LANES), lambda hp, i: (i, hp)),
                  pl.BlockSpec((t, LANES), lambda hp, i: (0, hp)),
                  pl.BlockSpec((2, nb, HEAD_DIM, MOBA_BLOCK), lambda hp, i: (hp, 0, 0, 0)),
                  pl.BlockSpec((LANES, LANES), lambda hp, i: (0, hp))],
        out_specs=pl.BlockSpec((MOBA_BLOCK, LANES), lambda hp, i: (i, hp)),
        out_shape=jax.ShapeDtypeStruct((t, ATT_WIDTH), BF16),
        compiler_params=_cparams(2), name="moba",
    )(qp, kp, vt, km)


def _dec_scores_kernel(bps, pt_ref, qb_ref, *refs):
    del pt_ref
    k_refs = refs[:2 * bps]
    s_ref, sel_ref, gate_s = refs[2 * bps:]
    step = pl.program_id(1)
    blk_lane = lax.broadcasted_iota(jnp.int32, (N_HEADS, LANES), 1)

    @pl.when(step == 0)
    def _():
        gate_s[...] = jnp.full_like(gate_s, -jnp.inf)

    qb = qb_ref[0]
    for bi in range(bps):
        s = jnp.concatenate([jnp.sum(k_refs[2 * bi + w][0] * qb, axis=1) for w in range(2)], axis=1)
        s_ref[0, :, bi * MOBA_BLOCK:(bi + 1) * MOBA_BLOCK] = s
        gate_s[...] = jnp.where(blk_lane == step * bps + bi, jnp.mean(s, axis=1, keepdims=True), gate_s[...])

    @pl.when(step == pl.num_programs(1) - 1)
    def _():
        g = gate_s[...]
        lane_f = blk_lane.astype(F32)
        out = jnp.full((N_HEADS, LANES), -1.0, F32)
        for r in range(MOBA_TOPK):
            mx, hit = _first_max_onehot(g, lane_f, 1)
            idx = jnp.min(jnp.where(hit, lane_f, jnp.inf), axis=1, keepdims=True)
            out = jnp.where((blk_lane == r) & (mx > -jnp.inf), idx, out)
            g = jnp.where(hit, -jnp.inf, g)
        sel_ref[0] = out


def _dec_combine_kernel(pt_ref, sel_ref, q_ref, kn_ref, vn_ref, *refs):
    del pt_ref
    s_refs = refs[:MOBA_TOPK]
    v_refs = refs[MOBA_TOPK:3 * MOBA_TOPK]
    o_ref = refs[3 * MOBA_TOPK]
    b, h = pl.program_id(0), pl.program_id(1)
    valid = [sel_ref[b, MOBA_TOPK * h + r] >= 0 for r in range(MOBA_TOPK)]
    s_own = jnp.sum(kn_ref[0] * q_ref[0], axis=1, keepdims=True)
    scores = [jnp.where(valid[r], s_refs[r][0, pl.ds(h, 1), :], -jnp.inf) for r in range(MOBA_TOPK)]
    big = functools.reduce(jnp.maximum, [jnp.max(sc, axis=1, keepdims=True) for sc in scores] + [s_own])
    w_own = jnp.exp(s_own - big)
    den = w_own
    num = w_own * vn_ref[0]
    for r in range(MOBA_TOPK):
        p = jnp.exp(scores[r] - big)
        den = den + jnp.sum(p, axis=1, keepdims=True)
        pv = sum(v_refs[2 * r + w][0] * p[:, None, w * PAGE_SIZE:(w + 1) * PAGE_SIZE] for w in range(2))
        num = num + jnp.sum(pv, axis=2)
    o_ref[0] = num / den


def _moba_decode(q_s, k_new, v_new, cache_k, cache_v, page_table):
    b = q_s.shape[0]
    n_pages = page_table.shape[1]
    pages_per_blk = MOBA_BLOCK // PAGE_SIZE
    assert pages_per_blk == 2 and n_pages % pages_per_blk == 0
    nbp = n_pages // pages_per_blk
    assert nbp <= LANES
    bps = next(c for c in (8, 4, 2, 1) if nbp % c == 0)
    past = nbp * MOBA_BLOCK
    ck = jnp.transpose(cache_k, (0, 2, 3, 1))
    cv = jnp.transpose(cache_v, (0, 2, 3, 1))
    q_lanes = jnp.broadcast_to(q_s.reshape(b, N_HEADS, HEAD_DIM)[..., None], (b, N_HEADS, HEAD_DIM, LANES))
    page = lambda j: pl.BlockSpec((1, N_HEADS, HEAD_DIM, PAGE_SIZE),
                                  lambda s, n, pt: (pt[s, pages_per_blk * bps * n + j], 0, 0, 0))
    pages = [page(j) for j in range(pages_per_blk * bps)]
    scores, sel = pl.pallas_call(
        functools.partial(_dec_scores_kernel, bps),
        grid_spec=pltpu.PrefetchScalarGridSpec(
            num_scalar_prefetch=1, grid=(b, nbp // bps),
            in_specs=[pl.BlockSpec((1, N_HEADS, HEAD_DIM, LANES), lambda s, n, pt: (s, 0, 0, 0))] + pages,
            out_specs=[pl.BlockSpec((1, N_HEADS, bps * MOBA_BLOCK), lambda s, n, pt: (s, 0, n)),
                       pl.BlockSpec((1, N_HEADS, LANES), lambda s, n, pt: (s, 0, 0))],
            scratch_shapes=[pltpu.VMEM((N_HEADS, LANES), F32)]),
        out_shape=[jax.ShapeDtypeStruct((b, N_HEADS, past), F32), jax.ShapeDtypeStruct((b, N_HEADS, LANES), F32)],
        compiler_params=_cparams(2), name="dec_scores",
    )(page_table, q_lanes, *([ck] * len(pages)))

    sel_i = sel[:, :, :MOBA_TOPK].astype(jnp.int32).reshape(b, N_HEADS * MOBA_TOPK)
    rows = lambda a: a.reshape(b * N_HEADS, 1, HEAD_DIM)
    pick = lambda s, h, sl, r: jnp.maximum(sl[s, MOBA_TOPK * h + r], 0)
    row_spec = pl.BlockSpec((1, 1, HEAD_DIM), lambda s, h, pt, sl: (s * N_HEADS + h, 0, 0))
    score_spec = lambda r: pl.BlockSpec((1, N_HEADS, MOBA_BLOCK), lambda s, h, pt, sl: (s, 0, pick(s, h, sl, r)))
    value_spec = lambda r, w: pl.BlockSpec(
        (1, 1, HEAD_DIM, PAGE_SIZE), lambda s, h, pt, sl: (pt[s, pages_per_blk * pick(s, h, sl, r) + w], h, 0, 0))
    value_specs = [value_spec(r, w) for r in range(MOBA_TOPK) for w in range(pages_per_blk)]
    out = pl.pallas_call(
        _dec_combine_kernel,
        grid_spec=pltpu.PrefetchScalarGridSpec(
            num_scalar_prefetch=2, grid=(b, N_HEADS),
            in_specs=[row_spec, row_spec, row_spec] + [score_spec(r) for r in range(MOBA_TOPK)] + value_specs,
            out_specs=row_spec),
        out_shape=jax.ShapeDtypeStruct((b * N_HEADS, 1, HEAD_DIM), F32),
        compiler_params=_cparams(2), name="dec_combine",
    )(page_table, sel_i, rows(q_s), rows(k_new), rows(v_new),
      *([scores] * MOBA_TOPK), *([cv] * len(value_specs)))
    return out.reshape(b, ATT_WIDTH)


def _post_conv(y, attn_bf, x, b_dw, ln_g, ln_b, w_out):
    y = y + b_dw
    mu = jnp.mean(y, axis=-1, keepdims=True)
    yc = y - mu
    var = jnp.mean(yc * yc, axis=-1, keepdims=True)
    y = yc * lax.rsqrt(var + NORM_EPS) * ln_g + ln_b
    y = y * _sigmoid(y)
    cat = jnp.concatenate([attn_bf, y.astype(BF16)], axis=1)
    return x + jnp.dot(cat, w_out, preferred_element_type=F32)


def _mix_kernel(glu_ref, halo_ref, attn_ref, x_ref, wdw_ref, bdw_ref, lng_ref, lnb_ref, wout_ref, h_ref, xin):
    tm = glu_ref.shape[0]
    first = pl.program_id(0) == 0
    xin[0:CONV_HALO, :] = jnp.where(first, 0.0, halo_ref[...])
    xin[CONV_HALO:CONV_HALO + tm, :] = glu_ref[...]
    xin[CONV_HALO + tm:, :] = jnp.zeros((SUBLANES, CONV_CH), F32)
    off = CONV_HALO - (CONV_K - 1)
    y = jnp.zeros((tm, CONV_CH), F32)
    for b in range(SUBLANES):
        z = None
        for a in range((CONV_HALO + SUBLANES) // SUBLANES):
            k = SUBLANES * a + b - off
            if 0 <= k < CONV_K:
                term = wdw_ref[k:k + 1, :] * xin[SUBLANES * a:SUBLANES * a + tm + SUBLANES, :]
                z = term if z is None else z + term
        if z is not None:
            y = y + z[b:b + tm]
    h_ref[...] = _post_conv(y, attn_ref[...], x_ref[...], bdw_ref[...], lng_ref[...], lnb_ref[...], wout_ref[...])


def _mix_prompt(glu, attn_bf, x, w_dw, b_dw, ln_g, ln_b, w_out_bf, *, tm):
    t = glu.shape[0]
    assert tm % CONV_HALO == 0 and t % tm == 0
    ratio = tm // CONV_HALO
    const = lambda shape: pl.BlockSpec(shape, lambda i: (0,) * len(shape))
    return pl.pallas_call(
        _mix_kernel,
        grid=(t // tm,),
        in_specs=[pl.BlockSpec((tm, CONV_CH), lambda i: (i, 0)),
                  pl.BlockSpec((CONV_HALO, CONV_CH), lambda i: (jnp.maximum(i * ratio - 1, 0), 0)),
                  pl.BlockSpec((tm, ATT_WIDTH), lambda i: (i, 0)),
                  pl.BlockSpec((tm, D_MODEL), lambda i: (i, 0)),
                  const((CONV_HALO, CONV_CH)), const((1, CONV_CH)), const((1, CONV_CH)), const((1, CONV_CH)),
                  const((D_MODEL, D_MODEL))],
        out_specs=pl.BlockSpec((tm, D_MODEL), lambda i: (i, 0)),
        out_shape=jax.ShapeDtypeStruct((t, D_MODEL), F32),
        scratch_shapes=[pltpu.VMEM((tm + CONV_HALO + SUBLANES, CONV_CH), F32)],
        compiler_params=_cparams(1), name="mix",
    )(glu, glu, attn_bf, x, w_dw, b_dw, ln_g, ln_b, w_out_bf)


def _mix_dec_kernel(xin_ref, attn_ref, x_ref, wdw_ref, bdw_ref, lng_ref, lnb_ref, wout_ref, h_ref):
    y = jnp.sum(xin_ref[...] * wdw_ref[...][None], axis=1)
    h_ref[...] = _post_conv(y, attn_ref[...].astype(BF16), x_ref[...], bdw_ref[...], lng_ref[...],
                            lnb_ref[...], wout_ref[...])


def _mix_decode(xin, attn, x, w_dw, b_dw, ln_g, ln_b, w_out_bf):
    b = x.shape[0]
    full = lambda a: pl.BlockSpec(a.shape, lambda i: (0,) * a.ndim)
    args = (xin, attn, x, w_dw, b_dw, ln_g, ln_b, w_out_bf)
    return pl.pallas_call(
        _mix_dec_kernel, grid=(1,), in_specs=[full(a) for a in args],
        out_specs=pl.BlockSpec((b, D_MODEL), lambda i: (0, 0)),
        out_shape=jax.ShapeDtypeStruct((b, D_MODEL), F32),
        compiler_params=_cparams(1), name="mix_dec",
    )(*args)


def _hyperbola_cells():
    return [(i, j) for i in range(PEER_TOPK) for j in range(PEER_TOPK) if (i + 1) * (j + 1) <= PEER_TOPK]


def _route_kernel(h_ref, g_ref, wq_ref, keys_ref, xn_ref, nc1_ref, r2_ref, e1_ref, e2_ref,
                  sc_s, rank_s, top_s, cnt_s, z_s):
    tm = h_ref.shape[0]
    nk = PEER_NKEYS
    xn = _rmsnorm(h_ref[...], g_ref[...]).astype(BF16)
    xn_ref[...] = xn
    qb = jnp.dot(xn, wq_ref[...], preferred_element_type=F32).astype(BF16)
    for hc in range(2 * PEER_HEADS):
        sc_s[hc] = _dot_nt(keys_ref[hc], qb[:, hc * nk:(hc + 1) * nk])

    key_f = lax.broadcasted_iota(jnp.int32, (nk, tm), 0).astype(F32)
    sub = lax.broadcasted_iota(jnp.int32, (PEER_HEADS, tm), 0)
    cells = _hyperbola_cells()

    def rank_rounds(hcs, exact_ties):
        xs = [sc_s[hc] for hc in hcs]
        ranks = [jnp.full((nk, tm), float(PEER_TOPK), F32) for _ in hcs]
        for r in range(PEER_TOPK):
            for n, hc in enumerate(hcs):
                if exact_ties:
                    mx, hit = _first_max_onehot(xs[n], key_f, 0)
                else:
                    mx = jnp.max(xs[n], axis=0, keepdims=True)
                    hit = xs[n] == mx
                ranks[n] = jnp.where(hit, float(r), ranks[n])
                xs[n] = jnp.where(hit, -jnp.inf, xs[n])
                top_s[hc, r] = jnp.broadcast_to(mx, (8, tm))
        for n, hc in enumerate(hcs):
            rank_s[hc] = ranks[n]
        return ranks

    def rank_body(h, carry):
        hcs = (2 * h, 2 * h + 1)
        ranks = rank_rounds(hcs, exact_ties=False)
        ranked = sum(jnp.sum(jnp.where(rk < float(PEER_TOPK), 1.0, 0.0), axis=0, keepdims=True) for rk in ranks)
        tied = jnp.max(jnp.abs(ranked - float(len(hcs) * PEER_TOPK))) > 0.0

        @pl.when(tied)
        def _():
            rank_rounds(hcs, exact_ties=True)

        return carry

    lax.fori_loop(0, PEER_HEADS, rank_body, 0)

    def heads_on_sublanes(side, r):
        out = top_s[side, r]
        for h in range(1, PEER_HEADS):
            out = jnp.where(sub == h, top_s[2 * h + side, r], out)
        return out

    a = [heads_on_sublanes(0, r) for r in range(PEER_TOPK)]
    bb = [heads_on_sublanes(1, r) for r in range(PEER_TOPK)]
    cur0 = tuple(a[i] + bb[j] for (i, j) in cells)
    best = cur0[0]

    def pick_body(_, carry):
        cur, taken = carry
        mx = functools.reduce(jnp.maximum, cur)
        first = functools.reduce(jnp.minimum, [jnp.where(cur[ci] == mx, float(ci), jnp.inf)
                                               for ci in range(len(cells))])
        hits = [first == float(ci) for ci in range(len(cells))]
        return (tuple(jnp.where(hit, -jnp.inf, c) for hit, c in zip(hits, cur)),
                tuple(jnp.where(hit, 1.0, t) for hit, t in zip(hits, taken)))

    state = (cur0, tuple(jnp.zeros_like(best) for _ in cells))
    for r in range(PEER_TOPK):
        state = pick_body(r, state)
    taken = state[1]
    z = jnp.zeros_like(best)
    counts = [jnp.zeros_like(best) for _ in range(PEER_TOPK)]
    for ci, (i, j) in enumerate(cells):
        counts[i] = counts[i] + taken[ci]
        z = z + taken[ci] * jnp.exp(a[i] + bb[j] - best)
    for h in range(PEER_HEADS):
        z_s[h] = jnp.broadcast_to(z[h:h + 1], (8, tm))
        for r in range(PEER_TOPK):
            cnt_s[h, r] = jnp.broadcast_to(counts[r][h:h + 1], (8, tm))

    def gate_body(h, carry):
        r1 = rank_s[2 * h]
        nc = jnp.zeros((nk, tm), F32)
        for r in range(PEER_TOPK):
            nc = jnp.where(r1 == float(r), cnt_s[h, r, 0:1, :], nc)
        nc1_ref[h] = nc
        r2_ref[h] = rank_s[2 * h + 1].astype(r2_ref.dtype)
        a0 = top_s[2 * h, 0, 0:1, :]
        b0 = top_s[2 * h + 1, 0, 0:1, :]
        e1_ref[h] = jnp.exp(sc_s[2 * h] - a0) / z_s[h, 0:1, :]
        e2_ref[h] = jnp.exp(sc_s[2 * h + 1] - b0).astype(e2_ref.dtype)
        return carry

    lax.fori_loop(0, PEER_HEADS, gate_body, 0)


def _peer_route(h1, g_ffn, wq_bf, keys_bf, *, tm):
    t = h1.shape[0]
    assert t % tm == 0 and tm % LANES == 0
    const = lambda shape: pl.BlockSpec(shape, lambda i: (0,) * len(shape))
    gate_shape = lambda dt: jax.ShapeDtypeStruct((PEER_HEADS, PEER_NKEYS, t), dt)
    gate_spec = pl.BlockSpec((PEER_HEADS, PEER_NKEYS, tm), lambda i: (0, 0, i))
    return pl.pallas_call(
        _route_kernel,
        grid=(t // tm,),
        in_specs=[pl.BlockSpec((tm, D_MODEL), lambda i: (i, 0)), const((1, D_MODEL)),
                  const(wq_bf.shape), const(keys_bf.shape)],
        out_specs=[pl.BlockSpec((tm, D_MODEL), lambda i: (i, 0))] + [gate_spec] * 4,
        out_shape=[jax.ShapeDtypeStruct((t, D_MODEL), BF16),
                   gate_shape(F32), gate_shape(BF16), gate_shape(F32), gate_shape(BF16)],
        scratch_shapes=[pltpu.VMEM((2 * PEER_HEADS, PEER_NKEYS, tm), F32),
                        pltpu.VMEM((2 * PEER_HEADS, PEER_NKEYS, tm), F32),
                        pltpu.VMEM((2 * PEER_HEADS, PEER_TOPK, 8, tm), F32),
                        pltpu.VMEM((PEER_HEADS, PEER_TOPK, 8, tm), F32),
                        pltpu.VMEM((PEER_HEADS, 8, tm), F32)],
        compiler_params=_cparams(1), name="route",
    )(h1, g_ffn, wq_bf, keys_bf)


def _experts_kernel(xn_ref, u_ref, vt_ref, nc1_ref, r2_ref, e1_ref, e2_ref, h_ref, o_ref, acc, w_s):
    c = pl.program_id(1)
    last = pl.num_programs(1) - 1

    @pl.when(c == 0)
    def _():
        acc[...] = jnp.zeros_like(acc)
        w_s[...] = jnp.zeros_like(w_s)

    acc[...] += jnp.dot(vt_ref[...], w_s[...], preferred_element_type=F32)

    blk = jnp.minimum(c, last - 1)
    pre = _dot_nt(u_ref[...], xn_ref[...])
    act = (0.5 * pre * (1.0 + lax.erf(pre * (2.0 ** -0.5)))).astype(BF16)
    n_i1 = EXPERT_BLOCK // PEER_NKEYS
    zero = jnp.zeros((PEER_NKEYS, act.shape[1]), BF16)
    gates = [zero] * n_i1
    for h in range(PEER_HEADS):
        r2 = r2_ref[h]
        e2 = e2_ref[h]
        for u in range(n_i1):
            i1 = blk * n_i1 + u
            nc = nc1_ref[h, pl.ds(i1, 1), :].astype(BF16)
            e1 = e1_ref[h, pl.ds(i1, 1), :].astype(BF16)
            gates[u] = gates[u] + jnp.where(r2 < nc, e2, zero) * e1
    w_s[...] = jnp.concatenate(gates, axis=0) * act

    @pl.when(c == last)
    def _():
        o_ref[...] = h_ref[...] + acc[...].T


def _peer_experts(xn, u_bf, vt_bf, nc1, r2, e1, e2, h1, *, tm):
    t = xn.shape[0]
    n_exp = u_bf.shape[0]
    assert t % tm == 0 and n_exp % EXPERT_BLOCK == 0
    n_blk = n_exp // EXPERT_BLOCK
    once = dict(pipeline_mode=pl.Buffered(1))
    gate_spec = pl.BlockSpec((PEER_HEADS, PEER_NKEYS, tm), lambda i, c: (0, 0, i), **once)
    tok_spec = pl.BlockSpec((tm, D_MODEL), lambda i, c: (i, 0))
    return pl.pallas_call(
        _experts_kernel,
        grid=(t // tm, n_blk + 1),
        in_specs=[tok_spec,
                  pl.BlockSpec((EXPERT_BLOCK, D_MODEL), lambda i, c: (jnp.minimum(c, n_blk - 1), 0)),
                  pl.BlockSpec((D_MODEL, EXPERT_BLOCK), lambda i, c: (0, jnp.maximum(c - 1, 0))),
                  gate_spec, gate_spec, gate_spec, gate_spec,
                  pl.BlockSpec((tm, D_MODEL), lambda i, c: (i, 0), **once)],
        out_specs=tok_spec,
        out_shape=jax.ShapeDtypeStruct((t, D_MODEL), F32),
        scratch_shapes=[pltpu.VMEM((D_MODEL, tm), F32), pltpu.VMEM((EXPERT_BLOCK, tm), BF16)],
        compiler_params=_cparams(2), name="experts",
    )(xn, u_bf, vt_bf, nc1, r2, e1, e2, h1)


def _ple_kernel(h_ref, p_ref, gple_ref, wg_ref, wp_ref, gfin_ref, y_ref):
    h = h_ref[...]
    gate = _sigmoid(jnp.dot(_rmsnorm(h, gple_ref[...]).astype(BF16), wg_ref[...], preferred_element_type=F32))
    proj = jnp.dot(p_ref[...].astype(BF16), wp_ref[...], preferred_element_type=F32)
    y_ref[...] = _rmsnorm(h + gate * proj, gfin_ref[...])


def _ple_final(h2, p_emb, g_ple, wg_bf, wp_bf, g_final, *, tm):
    t = h2.shape[0]
    const = lambda shape: pl.BlockSpec(shape, lambda i: (0,) * len(shape))
    return pl.pallas_call(
        _ple_kernel,
        grid=(t // tm,),
        in_specs=[pl.BlockSpec((tm, D_MODEL), lambda i: (i, 0)), pl.BlockSpec((tm, PLE_DIM), lambda i: (i, 0)),
                  const((1, D_MODEL)), const((D_MODEL, D_MODEL)), const((PLE_DIM, D_MODEL)), const((1, D_MODEL))],
        out_specs=pl.BlockSpec((tm, D_MODEL), lambda i: (i, 0)),
        out_shape=jax.ShapeDtypeStruct((t, D_MODEL), F32),
        compiler_params=_cparams(1), name="ple",
    )(h2, p_emb, g_ple, wg_bf, wp_bf, g_final)


def _pick_tile(t, want):
    tm = min(want, t)
    assert t % tm == 0
    return tm


def _ffn(h1, p_emb, lw, g_final):
    t = h1.shape[0]
    xn, nc1, r2, e1, e2 = _peer_route(h1, lw["g_ffn"], lw["wq"], lw["keys"], tm=_pick_tile(t, 256))
    h2 = _peer_experts(xn, lw["u"], lw["vt"], nc1, r2, e1, e2, h1, tm=_pick_tile(t, 1024))
    return _ple_final(h2, p_emb, lw["g_ple"], lw["w_ple_gate"], lw["w_ple_proj"], g_final, tm=_pick_tile(t, 256))


def kernel(x_prompt, x_sample, cache_k, cache_v, cache_conv, page_table, p_prompt, p_sample,
           g_mix, w_in, w_dw, b_dw, ln_g, ln_b, w_out, g_ffn, peer_wq, peer_keys, peer_u, peer_v,
           g_ple, w_ple_gate, w_ple_proj, g_final):
    depth = g_mix.shape[0]
    batch, seq, _ = x_prompt.shape
    dec_batch, dec_seq, _ = x_sample.shape
    assert depth == 1 and batch == 1 and dec_seq == 1
    past_len = page_table.shape[1] * PAGE_SIZE
    assert seq % MOBA_BLOCK == 0 and past_len % MOBA_BLOCK == 0 and seq >= CONV_K - 1
    row = lambda a: a.reshape(1, -1)
    lane = jnp.arange(LANES) % (ROT_DIM // 2)
    inv_lane = (ROPE_THETA ** (-lane.astype(F32) * 2.0 / ROT_DIM)).reshape(1, LANES)
    l = 0
    lw = dict(
        g_ffn=row(g_ffn[l]), wq=peer_wq[l].astype(BF16),
        keys=peer_keys[l].reshape(2 * PEER_HEADS, PEER_NKEYS, -1).astype(BF16),
        u=peer_u[l].astype(BF16), vt=peer_v[l].T.astype(BF16),
        g_ple=row(g_ple[l]), w_ple_gate=w_ple_gate[l].astype(BF16), w_ple_proj=w_ple_proj[l].astype(BF16))
    w_in_bf = w_in[l].astype(BF16)
    w_out_bf = w_out[l].astype(BF16)
    w_dw_pad = jnp.pad(w_dw[l], ((0, CONV_HALO - CONV_K), (0, 0)))
    conv_args = (w_dw_pad, row(b_dw[l]), row(ln_g[l]), row(ln_b[l]), w_out_bf)
    g_fin = row(g_final)

    xp = x_prompt.reshape(seq, D_MODEL)
    k_p, v_p, glu_p, qp, kp, vt, kmean = _inproj(xp, row(g_mix[l]), w_in_bf, inv_lane, pos0=0, pos_stride=1,
                                                 tm=MOBA_BLOCK, with_aux=True)
    attn_p = _moba_prompt(qp, kp, vt, kmean)
    h1_p = _mix_prompt(glu_p, attn_p, xp, *conv_args, tm=_pick_tile(seq, 256))
    y_p = _ffn(h1_p, p_prompt[l].reshape(seq, PLE_DIM), lw, g_fin)

    xs = x_sample.reshape(dec_batch, D_MODEL)
    k_s, v_s, glu_s, q_s = _inproj(xs, row(g_mix[l]), w_in_bf, inv_lane, pos0=past_len, pos_stride=0,
                                   tm=dec_batch, with_aux=False)
    attn_s = _moba_decode(q_s, k_s, v_s, cache_k[l], cache_v[l], page_table)
    xin_s = jnp.concatenate([cache_conv[l], glu_s[:, None, :]], axis=1)
    xin_pad = jnp.pad(xin_s, ((0, 0), (0, CONV_HALO - CONV_K), (0, 0)))
    h1_s = _mix_decode(xin_pad, attn_s, xs, *conv_args)
    pad = (-dec_batch) % LANES
    h1_s = jnp.pad(h1_s, ((0, pad), (0, 0)))
    p_s = jnp.pad(p_sample[l].reshape(dec_batch, PLE_DIM), ((0, pad), (0, 0)))
    y_s = _ffn(h1_s, p_s, lw, g_fin)[:dec_batch]

    conv_p = glu_p[seq - (CONV_K - 1):].reshape(1, 1, CONV_K - 1, CONV_CH)
    conv_s = xin_s[:, 1:].reshape(1, dec_batch, CONV_K - 1, CONV_CH)
    return (y_p.reshape(1, seq, D_MODEL), y_s.reshape(dec_batch, 1, D_MODEL),
            k_p.reshape(1, 1, seq, N_HEADS, HEAD_DIM), v_p.reshape(1, 1, seq, N_HEADS, HEAD_DIM), conv_p,
            k_s.reshape(1, dec_batch, 1, N_HEADS, HEAD_DIM), v_s.reshape(1, dec_batch, 1, N_HEADS, HEAD_DIM),
            conv_s)
```

```python
import functools

import jax
import jax.numpy as jnp
from jax import lax
from jax.experimental import pallas as pl
from jax.experimental.pallas import tpu as pltpu

F32 = jnp.float32
BF16 = jnp.bfloat16

D_MODEL = 1024
N_HEADS = 8
HEAD_DIM = 64
ATT_WIDTH = N_HEADS * HEAD_DIM
CONV_CH = D_MODEL - ATT_WIDTH
ROT_DIM = HEAD_DIM // 4
ROPE_THETA = 500000.0
MOBA_BLOCK = 256
MOBA_TOPK = 3
CONV_K = 31
PAGE_SIZE = 128
PEER_HEADS = 8
PEER_NKEYS = 128
PEER_TOPK = 16
PLE_DIM = 256
NORM_EPS = 1e-6

LANES = 128
SUBLANES = 8
CONV_HALO = 32
MASKED = -1e30
EXPERT_BLOCK = 8 * PEER_NKEYS
VMEM_LIMIT = 56 * 1024 * 1024
LOG2_E = 1.4426950408889634


def _cparams(n_axes):
    return pltpu.CompilerParams(dimension_semantics=("arbitrary",) * n_axes, vmem_limit_bytes=VMEM_LIMIT)


def _rmsnorm(x, g):
    return x * lax.rsqrt(jnp.mean(x * x, axis=-1, keepdims=True) + NORM_EPS) * g


def _sigmoid(x):
    return 1.0 / (1.0 + jnp.exp(-x))


def _dot_nt(a, b):
    return lax.dot_general(a, b, (((1,), (1,)), ((), ())), preferred_element_type=F32)


def _rope(x, cos4, sin4, low4):
    half = ROT_DIM // 2
    from_hi = pltpu.roll(x, ATT_WIDTH - half, 1)
    from_lo = pltpu.roll(x, half, 1)
    return x * cos4 + jnp.where(low4, from_hi, from_lo) * sin4


def _inproj_kernel(pos0, pos_stride, with_aux, x_ref, g_ref, w_ref, inv_ref, *out_refs):
    if with_aux:
        k_ref, v_ref, glu_ref, q_ref, kp_ref, vt_ref, kmean_ref = out_refs
    else:
        k_ref, v_ref, glu_ref, q_ref = out_refs
    tm = x_ref.shape[0]
    xn = _rmsnorm(x_ref[...], g_ref[...]).astype(BF16)
    z = jnp.dot(xn, w_ref[...], preferred_element_type=F32)
    q = z[:, 0 * ATT_WIDTH:1 * ATT_WIDTH]
    k = z[:, 1 * ATT_WIDTH:2 * ATT_WIDTH]
    v = z[:, 2 * ATT_WIDTH:3 * ATT_WIDTH]
    ga = z[:, 3 * ATT_WIDTH:3 * ATT_WIDTH + CONV_CH]
    gb = z[:, 3 * ATT_WIDTH + CONV_CH:]

    row = lax.broadcasted_iota(jnp.int32, (tm, LANES), 0)
    d = lax.broadcasted_iota(jnp.int32, (tm, LANES), 1) % HEAD_DIM
    pos = (pos0 + (pl.program_id(0) * tm + row) * pos_stride).astype(F32)
    ang = pos * inv_ref[...]
    cos = jnp.where(d < ROT_DIM, jnp.cos(ang), 1.0)
    sin = jnp.sin(ang)
    sin = jnp.where(d < ROT_DIM // 2, -sin, jnp.where(d < ROT_DIM, sin, 0.0))
    reps = ATT_WIDTH // LANES
    cos4 = jnp.concatenate([cos] * reps, axis=1)
    sin4 = jnp.concatenate([sin] * reps, axis=1)
    low4 = lax.broadcasted_iota(jnp.int32, (tm, ATT_WIDTH), 1) % HEAD_DIM < ROT_DIM // 2

    k = _rope(k, cos4, sin4, low4)
    q = _rope(q, cos4, sin4, low4) * (HEAD_DIM ** -0.5)
    k_ref[...] = k
    v_ref[...] = v
    glu_ref[...] = ga * _sigmoid(gb)
    if with_aux:
        q_ref[...] = (q * LOG2_E).astype(BF16)
        kp_ref[...] = k.astype(BF16)
        kmean_ref[0] = jnp.mean(k, axis=0, keepdims=True)
        vt = v.T
        for h in range(N_HEADS):
            vt_ref[h, 0] = vt[h * HEAD_DIM:(h + 1) * HEAD_DIM, :].astype(BF16)
    else:
        q_ref[...] = q


def _inproj(x, g_mix, w_in_bf, inv_lane, *, pos0, pos_stride, tm, with_aux):
    t = x.shape[0]
    n_in = w_in_bf.shape[1]
    grid = (t // tm,)
    row_spec = lambda w: pl.BlockSpec((tm, w), lambda i: (i, 0))
    const = lambda shape: pl.BlockSpec(shape, lambda i: (0,) * len(shape))
    out_shape = [jax.ShapeDtypeStruct((t, ATT_WIDTH), F32), jax.ShapeDtypeStruct((t, ATT_WIDTH), F32),
                 jax.ShapeDtypeStruct((t, CONV_CH), F32)]
    out_specs = [row_spec(ATT_WIDTH), row_spec(ATT_WIDTH), row_spec(CONV_CH)]
    if with_aux:
        assert tm == MOBA_BLOCK
        nb = t // MOBA_BLOCK
        out_shape += [jax.ShapeDtypeStruct((t, ATT_WIDTH), BF16), jax.ShapeDtypeStruct((t, ATT_WIDTH), BF16),
                      jax.ShapeDtypeStruct((N_HEADS, nb, HEAD_DIM, MOBA_BLOCK), BF16),
                      jax.ShapeDtypeStruct((nb, 1, ATT_WIDTH), F32)]
        out_specs += [row_spec(ATT_WIDTH), row_spec(ATT_WIDTH),
                      pl.BlockSpec((N_HEADS, 1, HEAD_DIM, MOBA_BLOCK), lambda i: (0, i, 0, 0)),
                      pl.BlockSpec((1, 1, ATT_WIDTH), lambda i: (i, 0, 0))]
    else:
        out_shape += [jax.ShapeDtypeStruct((t, ATT_WIDTH), F32)]
        out_specs += [row_spec(ATT_WIDTH)]
    return pl.pallas_call(
        functools.partial(_inproj_kernel, pos0, pos_stride, with_aux),
        grid=grid,
        in_specs=[row_spec(D_MODEL), const((1, D_MODEL)), const((D_MODEL, n_in)), const((1, LANES))],
        out_specs=out_specs, out_shape=out_shape, compiler_params=_cparams(1),
        name="inproj_aux" if with_aux else "inproj",
    )(x, g_mix, w_in_bf, inv_lane)


def _first_max_onehot(g, idx_f, axis):
    mx = jnp.max(g, axis=axis, keepdims=True)
    first = jnp.min(jnp.where(g == mx, idx_f, jnp.inf), axis=axis, keepdims=True)
    return mx, idx_f == first


def _moba_kernel(chunk, q_ref, kp_ref, vt_ref, kmean_ref, o_ref):
    i = pl.program_id(1)
    blk = MOBA_BLOCK
    q2 = q_ref[...]
    km = kmean_ref[...].astype(BF16)
    lane = lax.broadcasted_iota(jnp.int32, (blk, LANES), 1)
    lane_f = lane.astype(F32)
    key_row = lax.broadcasted_iota(jnp.int32, (blk, blk), 0)
    qry_col = lax.broadcasted_iota(jnp.int32, (blk, blk), 1)

    qzs, biases = [], []
    for hh in range(2):
        qz = jnp.where(lane // HEAD_DIM == hh, q2, jnp.zeros_like(q2))
        gate = _dot_nt(qz, km)
        g = jnp.where(lane < i, gate, -jnp.inf)
        sel = jnp.zeros((blk, LANES), jnp.bool_)
        for _ in range(MOBA_TOPK):
            mx, hit = _first_max_onehot(g, lane_f, 1)
            sel = sel | (hit & (mx > -jnp.inf))
            g = jnp.where(hit, -jnp.inf, g)
        qzs.append(qz)
        biases.append(jnp.where(sel, 0.0, MASKED))

    k_own = kp_ref[pl.ds(pl.multiple_of(i * blk, blk), blk), :]
    state = []
    for hh in range(2):
        s = _dot_nt(k_own, qzs[hh])
        s = jnp.where(key_row <= qry_col, s, MASKED)
        m = jnp.max(s, axis=0, keepdims=True)
        p = jnp.exp2(s - m)
        l = jnp.sum(p, axis=0, keepdims=True)
        acc = jnp.dot(vt_ref[hh, i], p.astype(BF16), preferred_element_type=F32)
        state += [m, l, acc]

    def flash_step(chunk, c, carry):
        base = c * chunk
        shift = lax.rem(LANES - base, LANES)
        k_augs = []
        for r in range(chunk):
            kr = kp_ref[pl.ds(pl.multiple_of((base + r) * blk, blk), blk), :]
            k_augs.append(jnp.concatenate([kr, (lane == r).astype(BF16)], axis=1))
        out = []
        for hh in range(2):
            m, l, acc = carry[3 * hh:3 * hh + 3]
            bias_c = pltpu.roll(biases[hh], shift, 1).astype(BF16)
            q_aug = jnp.concatenate([qzs[hh], bias_c], axis=1)
            ss = [_dot_nt(ka, q_aug) for ka in k_augs]
            mc = functools.reduce(jnp.maximum, [jnp.max(s, axis=0, keepdims=True) for s in ss])
            m_new = jnp.maximum(m, mc)
            alpha = jnp.exp2(m - m_new)
            l = alpha * l
            acc = alpha * acc
            for r in range(chunk):
                p = jnp.exp2(ss[r] - m_new)
                l = l + jnp.sum(p, axis=0, keepdims=True)
                acc = acc + jnp.dot(vt_ref[hh, base + r], p.astype(BF16), preferred_element_type=F32)
            out += [m_new, l, acc]
        return tuple(out)

    full, rest = i // chunk, i % chunk
    half = chunk // 2
    short_tail = (rest > 0) & (rest <= half) if half else jnp.bool_(False)
    n_long = full + ((rest > 0) & jnp.logical_not(short_tail)).astype(jnp.int32)
    state = lax.fori_loop(0, n_long, functools.partial(flash_step, chunk), tuple(state))
    if half:
        first_half = 2 * full
        state = lax.fori_loop(first_half, first_half + short_tail.astype(jnp.int32),
                              functools.partial(flash_step, half), state)
    o_t = jnp.concatenate([state[2] / state[1], state[5] / state[4]], axis=0)
    o_ref[...] = o_t.T.astype(o_ref.dtype)


def _moba_prompt(qp, kp, vt, kmean):
    t = qp.shape[0]
    nb = t // MOBA_BLOCK
    assert t % MOBA_BLOCK == 0 and nb <= LANES
    chunk = next(c for c in (8, 4, 2, 1) if nb % c == 0)
    km = jnp.pad(kmean.reshape(nb, ATT_WIDTH), ((0, LANES - nb), (0, 0)))
    return pl.pallas_call(
        functools.partial(_moba_kernel, chunk),
        grid=(N_HEADS // 2, nb),
        in_specs=[pl.BlockSpec((MOBA_BLOCK, LANES), lambda hp, i: (i, hp)),
                  pl.BlockSpec((t, LANES), lambda hp, i: (0, hp)),
                  pl.BlockSpec((2, nb, HEAD_DIM, MOBA_BLOCK), lambda hp, i: (hp, 0, 0, 0)),
                  pl.BlockSpec((LANES, LANES), lambda hp, i: (0, hp))],
        out_specs=pl.BlockSpec((MOBA_BLOCK, LANES), lambda hp, i: (i, hp)),
        out_shape=jax.ShapeDtypeStruct((t, ATT_WIDTH), BF16),
        compiler_params=_cparams(2), name="moba",
    )(qp, kp, vt, km)


def _dec_scores_kernel(bps, pt_ref, qb_ref, *refs):
    del pt_ref
    k_refs = refs[:2 * bps]
    s_ref, sel_ref, gate_s = refs[2 * bps:]
    step = pl.program_id(1)
    blk_lane = lax.broadcasted_iota(jnp.int32, (N_HEADS, LANES), 1)

    @pl.when(step == 0)
    def _():
        gate_s[...] = jnp.full_like(gate_s, -jnp.inf)

    qb = qb_ref[0]
    for bi in range(bps):
        s = jnp.concatenate([jnp.sum(k_refs[2 * bi + w][0] * qb, axis=1) for w in range(2)], axis=1)
        s_ref[0, :, bi * MOBA_BLOCK:(bi + 1) * MOBA_BLOCK] = s
        gate_s[...] = jnp.where(blk_lane == step * bps + bi, jnp.mean(s, axis=1, keepdims=True), gate_s[...])

    @pl.when(step == pl.num_programs(1) - 1)
    def _():
        g = gate_s[...]
        lane_f = blk_lane.astype(F32)
        out = jnp.full((N_HEADS, LANES), -1.0, F32)
        for r in range(MOBA_TOPK):
            mx, hit = _first_max_onehot(g, lane_f, 1)
            idx = jnp.min(jnp.where(hit, lane_f, jnp.inf), axis=1, keepdims=True)
            out = jnp.where((blk_lane == r) & (mx > -jnp.inf), idx, out)
            g = jnp.where(hit, -jnp.inf, g)
        sel_ref[0] = out


def _dec_combine_kernel(pt_ref, sel_ref, q_ref, kn_ref, vn_ref, *refs):
    del pt_ref
    s_refs = refs[:MOBA_TOPK]
    v_refs = refs[MOBA_TOPK:3 * MOBA_TOPK]
    o_ref = refs[3 * MOBA_TOPK]
    b, h = pl.program_id(0), pl.program_id(1)
    valid = [sel_ref[b, MOBA_TOPK * h + r] >= 0 for r in range(MOBA_TOPK)]
    s_own = jnp.sum(kn_ref[0] * q_ref[0], axis=1, keepdims=True)
    scores = [jnp.where(valid[r], s_refs[r][0, pl.ds(h, 1), :], -jnp.inf) for r in range(MOBA_TOPK)]
    big = functools.reduce(jnp.maximum, [jnp.max(sc, axis=1, keepdims=True) for sc in scores] + [s_own])
    w_own = jnp.exp(s_own - big)
    den = w_own
    num = w_own * vn_ref[0]
    for r in range(MOBA_TOPK):
        p = jnp.exp(scores[r] - big)
        den = den + jnp.sum(p, axis=1, keepdims=True)
        pv = sum(v_refs[2 * r + w][0] * p[:, None, w * PAGE_SIZE:(w + 1) * PAGE_SIZE] for w in range(2))
        num = num + jnp.sum(pv, axis=2)
    o_ref[0] = num / den


def _moba_decode(q_s, k_new, v_new, cache_k, cache_v, page_table):
    b = q_s.shape[0]
    n_pages = page_table.shape[1]
    pages_per_blk = MOBA_BLOCK // PAGE_SIZE
    assert pages_per_blk == 2 and n_pages % pages_per_blk == 0
    nbp = n_pages // pages_per_blk
    assert nbp <= LANES
    bps = next(c for c in (16, 8, 4, 2, 1) if nbp % c == 0)
    past = nbp * MOBA_BLOCK
    ck = jnp.transpose(cache_k, (0, 2, 3, 1))
    cv = jnp.transpose(cache_v, (0, 2, 3, 1))
    q_lanes = jnp.broadcast_to(q_s.reshape(b, N_HEADS, HEAD_DIM)[..., None], (b, N_HEADS, HEAD_DIM, LANES))
    page = lambda j: pl.BlockSpec((1, N_HEADS, HEAD_DIM, PAGE_SIZE),
                                  lambda s, n, pt: (pt[s, pages_per_blk * bps * n + j], 0, 0, 0))
    pages = [page(j) for j in range(pages_per_blk * bps)]
    scores, sel = pl.pallas_call(
        functools.partial(_dec_scores_kernel, bps),
        grid_spec=pltpu.PrefetchScalarGridSpec(
            num_scalar_prefetch=1, grid=(b, nbp // bps),
            in_specs=[pl.BlockSpec((1, N_HEADS, HEAD_DIM, LANES), lambda s, n, pt: (s, 0, 0, 0))] + pages,
            out_specs=[pl.BlockSpec((1, N_HEADS, bps * MOBA_BLOCK), lambda s, n, pt: (s, 0, n)),
                       pl.BlockSpec((1, N_HEADS, LANES), lambda s, n, pt: (s, 0, 0))],
            scratch_shapes=[pltpu.VMEM((N_HEADS, LANES), F32)]),
        out_shape=[jax.ShapeDtypeStruct((b, N_HEADS, past), F32), jax.ShapeDtypeStruct((b, N_HEADS, LANES), F32)],
        compiler_params=_cparams(2), name="dec_scores",
    )(page_table, q_lanes, *([ck] * len(pages)))

    sel_i = sel[:, :, :MOBA_TOPK].astype(jnp.int32).reshape(b, N_HEADS * MOBA_TOPK)
    rows = lambda a: a.reshape(b * N_HEADS, 1, HEAD_DIM)
    pick = lambda s, h, sl, r: jnp.maximum(sl[s, MOBA_TOPK * h + r], 0)
    row_spec = pl.BlockSpec((1, 1, HEAD_DIM), lambda s, h, pt, sl: (s * N_HEADS + h, 0, 0))
    score_spec = lambda r: pl.BlockSpec((1, N_HEADS, MOBA_BLOCK), lambda s, h, pt, sl: (s, 0, pick(s, h, sl, r)))
    value_spec = lambda r, w: pl.BlockSpec(
        (1, 1, HEAD_DIM, PAGE_SIZE), lambda s, h, pt, sl: (pt[s, pages_per_blk * pick(s, h, sl, r) + w], h, 0, 0))
    value_specs = [value_spec(r, w) for r in range(MOBA_TOPK) for w in range(pages_per_blk)]
    out = pl.pallas_call(
        _dec_combine_kernel,
        grid_spec=pltpu.PrefetchScalarGridSpec(
            num_scalar_prefetch=2, grid=(b, N_HEADS),
            in_specs=[row_spec, row_spec, row_spec] + [score_spec(r) for r in range(MOBA_TOPK)] + value_specs,
            out_specs=row_spec),
        out_shape=jax.ShapeDtypeStruct((b * N_HEADS, 1, HEAD_DIM), F32),
        compiler_params=_cparams(2), name="dec_combine",
    )(page_table, sel_i, rows(q_s), rows(k_new), rows(v_new),
      *([scores] * MOBA_TOPK), *([cv] * len(value_specs)))
    return out.reshape(b, ATT_WIDTH)


def _post_conv(y, attn_bf, x, b_dw, ln_g, ln_b, w_out):
    y = y + b_dw
    mu = jnp.mean(y, axis=-1, keepdims=True)
    yc = y - mu
    var = jnp.mean(yc * yc, axis=-1, keepdims=True)
    y = yc * lax.rsqrt(var + NORM_EPS) * ln_g + ln_b
    y = y * _sigmoid(y)
    cat = jnp.concatenate([attn_bf, y.astype(BF16)], axis=1)
    return x + jnp.dot(cat, w_out, preferred_element_type=F32)


def _mix_kernel(glu_ref, halo_ref, attn_ref, x_ref, wdw_ref, bdw_ref, lng_ref, lnb_ref, wout_ref, h_ref, xin):
    tm = glu_ref.shape[0]
    first = pl.program_id(0) == 0
    xin[0:CONV_HALO, :] = jnp.where(first, 0.0, halo_ref[...])
    xin[CONV_HALO:CONV_HALO + tm, :] = glu_ref[...]
    xin[CONV_HALO + tm:, :] = jnp.zeros((SUBLANES, CONV_CH), F32)
    off = CONV_HALO - (CONV_K - 1)
    y = jnp.zeros((tm, CONV_CH), F32)
    for b in range(SUBLANES):
        z = None
        for a in range((CONV_HALO + SUBLANES) // SUBLANES):
            k = SUBLANES * a + b - off
            if 0 <= k < CONV_K:
                term = wdw_ref[k:k + 1, :] * xin[SUBLANES * a:SUBLANES * a + tm + SUBLANES, :]
                z = term if z is None else z + term
        if z is not None:
            y = y + z[b:b + tm]
    h_ref[...] = _post_conv(y, attn_ref[...], x_ref[...], bdw_ref[...], lng_ref[...], lnb_ref[...], wout_ref[...])


def _mix_prompt(glu, attn_bf, x, w_dw, b_dw, ln_g, ln_b, w_out_bf, *, tm):
    t = glu.shape[0]
    assert tm % CONV_HALO == 0 and t % tm == 0
    ratio = tm // CONV_HALO
    const = lambda shape: pl.BlockSpec(shape, lambda i: (0,) * len(shape))
    return pl.pallas_call(
        _mix_kernel,
        grid=(t // tm,),
        in_specs=[pl.BlockSpec((tm, CONV_CH), lambda i: (i, 0)),
                  pl.BlockSpec((CONV_HALO, CONV_CH), lambda i: (jnp.maximum(i * ratio - 1, 0), 0)),
                  pl.BlockSpec((tm, ATT_WIDTH), lambda i: (i, 0)),
                  pl.BlockSpec((tm, D_MODEL), lambda i: (i, 0)),
                  const((CONV_HALO, CONV_CH)), const((1, CONV_CH)), const((1, CONV_CH)), const((1, CONV_CH)),
                  const((D_MODEL, D_MODEL))],
        out_specs=pl.BlockSpec((tm, D_MODEL), lambda i: (i, 0)),
        out_shape=jax.ShapeDtypeStruct((t, D_MODEL), F32),
        scratch_shapes=[pltpu.VMEM((tm + CONV_HALO + SUBLANES, CONV_CH), F32)],
        compiler_params=_cparams(1), name="mix",
    )(glu, glu, attn_bf, x, w_dw, b_dw, ln_g, ln_b, w_out_bf)


def _mix_dec_kernel(xin_ref, attn_ref, x_ref, wdw_ref, bdw_ref, lng_ref, lnb_ref, wout_ref, h_ref):
    y = jnp.sum(xin_ref[...] * wdw_ref[...][None], axis=1)
    h_ref[...] = _post_conv(y, attn_ref[...].astype(BF16), x_ref[...], bdw_ref[...], lng_ref[...],
                            lnb_ref[...], wout_ref[...])


def _mix_decode(xin, attn, x, w_dw, b_dw, ln_g, ln_b, w_out_bf):
    b = x.shape[0]
    full = lambda a: pl.BlockSpec(a.shape, lambda i: (0,) * a.ndim)
    args = (xin, attn, x, w_dw, b_dw, ln_g, ln_b, w_out_bf)
    return pl.pallas_call(
        _mix_dec_kernel, grid=(1,), in_specs=[full(a) for a in args],
        out_specs=pl.BlockSpec((b, D_MODEL), lambda i: (0, 0)),
        out_shape=jax.ShapeDtypeStruct((b, D_MODEL), F32),
        compiler_params=_cparams(1), name="mix_dec",
    )(*args)


def _hyperbola_cells():
    return [(i, j) for i in range(PEER_TOPK) for j in range(PEER_TOPK) if (i + 1) * (j + 1) <= PEER_TOPK]


def _route_kernel(h_ref, g_ref, wq_ref, keys_ref, xn_ref, nc1_ref, r2_ref, e1_ref, e2_ref,
                  sc_s, rank_s, top_s, cnt_s, z_s):
    tm = h_ref.shape[0]
    nk = PEER_NKEYS
    xn = _rmsnorm(h_ref[...], g_ref[...]).astype(BF16)
    xn_ref[...] = xn
    qb = jnp.dot(xn, wq_ref[...], preferred_element_type=F32).astype(BF16)
    for hc in range(2 * PEER_HEADS):
        sc_s[hc] = _dot_nt(keys_ref[hc], qb[:, hc * nk:(hc + 1) * nk])

    key_f = lax.broadcasted_iota(jnp.int32, (nk, tm), 0).astype(F32)
    sub = lax.broadcasted_iota(jnp.int32, (PEER_HEADS, tm), 0)
    cells = _hyperbola_cells()

    def rank_rounds(hcs, exact_ties):
        xs = [sc_s[hc] for hc in hcs]
        ranks = [jnp.full((nk, tm), float(PEER_TOPK), F32) for _ in hcs]
        for r in range(PEER_TOPK):
            for n, hc in enumerate(hcs):
                if exact_ties:
                    mx, hit = _first_max_onehot(xs[n], key_f, 0)
                else:
                    mx = jnp.max(xs[n], axis=0, keepdims=True)
                    hit = xs[n] == mx
                ranks[n] = jnp.where(hit, float(r), ranks[n])
                xs[n] = jnp.where(hit, -jnp.inf, xs[n])
                top_s[hc, r] = jnp.broadcast_to(mx, (8, tm))
        for n, hc in enumerate(hcs):
            rank_s[hc] = ranks[n]
        return ranks

    def rank_body(h, carry):
        hcs = (2 * h, 2 * h + 1)
        ranks = rank_rounds(hcs, exact_ties=False)
        ranked = sum(jnp.sum(jnp.where(rk < float(PEER_TOPK), 1.0, 0.0), axis=0, keepdims=True) for rk in ranks)
        tied = jnp.max(jnp.abs(ranked - float(len(hcs) * PEER_TOPK))) > 0.0

        @pl.when(tied)
        def _():
            rank_rounds(hcs, exact_ties=True)

        return carry

    lax.fori_loop(0, PEER_HEADS, rank_body, 0)

    def heads_on_sublanes(side, r):
        out = top_s[side, r]
        for h in range(1, PEER_HEADS):
            out = jnp.where(sub == h, top_s[2 * h + side, r], out)
        return out

    a = [heads_on_sublanes(0, r) for r in range(PEER_TOPK)]
    bb = [heads_on_sublanes(1, r) for r in range(PEER_TOPK)]
    cur0 = tuple(a[i] + bb[j] for (i, j) in cells)
    best = cur0[0]

    def pick_body(_, carry):
        cur, taken = carry
        mx = functools.reduce(jnp.maximum, cur)
        first = functools.reduce(jnp.minimum, [jnp.where(cur[ci] == mx, float(ci), jnp.inf)
                                               for ci in range(len(cells))])
        hits = [first == float(ci) for ci in range(len(cells))]
        return (tuple(jnp.where(hit, -jnp.inf, c) for hit, c in zip(hits, cur)),
                tuple(jnp.where(hit, 1.0, t) for hit, t in zip(hits, taken)))

    state = (cur0, tuple(jnp.zeros_like(best) for _ in cells))
    for r in range(PEER_TOPK):
        state = pick_body(r, state)
    taken = state[1]
    z = jnp.zeros_like(best)
    counts = [jnp.zeros_like(best) for _ in range(PEER_TOPK)]
    for ci, (i, j) in enumerate(cells):
        counts[i] = counts[i] + taken[ci]
        z = z + taken[ci] * jnp.exp(a[i] + bb[j] - best)
    for h in range(PEER_HEADS):
        z_s[h] = jnp.broadcast_to(z[h:h + 1], (8, tm))
        for r in range(PEER_TOPK):
            cnt_s[h, r] = jnp.broadcast_to(counts[r][h:h + 1], (8, tm))

    def gate_body(h, carry):
        r1 = rank_s[2 * h]
        nc = jnp.zeros((nk, tm), F32)
        for r in range(PEER_TOPK):
            nc = jnp.where(r1 == float(r), cnt_s[h, r, 0:1, :], nc)
        nc1_ref[h] = nc
        r2_ref[h] = rank_s[2 * h + 1].astype(r2_ref.dtype)
        a0 = top_s[2 * h, 0, 0:1, :]
        b0 = top_s[2 * h + 1, 0, 0:1, :]
        e1_ref[h] = jnp.exp(sc_s[2 * h] - a0) / z_s[h, 0:1, :]
        e2_ref[h] = jnp.exp(sc_s[2 * h + 1] - b0).astype(e2_ref.dtype)
        return carry

    lax.fori_loop(0, PEER_HEADS, gate_body, 0)


def _peer_route(h1, g_ffn, wq_bf, keys_bf, *, tm):
    t = h1.shape[0]
    assert t % tm == 0 and tm % LANES == 0
    const = lambda shape: pl.BlockSpec(shape, lambda i: (0,) * len(shape))
    gate_shape = lambda dt: jax.ShapeDtypeStruct((PEER_HEADS, PEER_NKEYS, t), dt)
    gate_spec = pl.BlockSpec((PEER_HEADS, PEER_NKEYS, tm), lambda i: (0, 0, i))
    return pl.pallas_call(
        _route_kernel,
        grid=(t // tm,),
        in_specs=[pl.BlockSpec((tm, D_MODEL), lambda i: (i, 0)), const((1, D_MODEL)),
                  const(wq_bf.shape), const(keys_bf.shape)],
        out_specs=[pl.BlockSpec((tm, D_MODEL), lambda i: (i, 0))] + [gate_spec] * 4,
        out_shape=[jax.ShapeDtypeStruct((t, D_MODEL), BF16),
                   gate_shape(F32), gate_shape(BF16), gate_shape(F32), gate_shape(BF16)],
        scratch_shapes=[pltpu.VMEM((2 * PEER_HEADS, PEER_NKEYS, tm), F32),
                        pltpu.VMEM((2 * PEER_HEADS, PEER_NKEYS, tm), F32),
                        pltpu.VMEM((2 * PEER_HEADS, PEER_TOPK, 8, tm), F32),
                        pltpu.VMEM((PEER_HEADS, PEER_TOPK, 8, tm), F32),
                        pltpu.VMEM((PEER_HEADS, 8, tm), F32)],
        compiler_params=_cparams(1), name="route",
    )(h1, g_ffn, wq_bf, keys_bf)


def _experts_kernel(xn_ref, u_ref, vt_ref, nc1_ref, r2_ref, e1_ref, e2_ref, h_ref, o_ref, acc, w_s):
    c = pl.program_id(1)
    last = pl.num_programs(1) - 1

    @pl.when(c == 0)
    def _():
        acc[...] = jnp.zeros_like(acc)
        w_s[...] = jnp.zeros_like(w_s)

    acc[...] += jnp.dot(vt_ref[...], w_s[...], preferred_element_type=F32)

    blk = jnp.minimum(c, last - 1)
    pre = _dot_nt(u_ref[...], xn_ref[...])
    act = (0.5 * pre * (1.0 + lax.erf(pre * (2.0 ** -0.5)))).astype(BF16)
    n_i1 = EXPERT_BLOCK // PEER_NKEYS
    zero = jnp.zeros((PEER_NKEYS, act.shape[1]), BF16)
    gates = [zero] * n_i1
    for h in range(PEER_HEADS):
        r2 = r2_ref[h]
        e2 = e2_ref[h]
        for u in range(n_i1):
            i1 = blk * n_i1 + u
            nc = nc1_ref[h, pl.ds(i1, 1), :].astype(BF16)
            e1 = e1_ref[h, pl.ds(i1, 1), :].astype(BF16)
            gates[u] = gates[u] + jnp.where(r2 < nc, e2, zero) * e1
    w_s[...] = jnp.concatenate(gates, axis=0) * act

    @pl.when(c == last)
    def _():
        o_ref[...] = h_ref[...] + acc[...].T


def _peer_experts(xn, u_bf, vt_bf, nc1, r2, e1, e2, h1, *, tm):
    t = xn.shape[0]
    n_exp = u_bf.shape[0]
    assert t % tm == 0 and n_exp % EXPERT_BLOCK == 0
    n_blk = n_exp // EXPERT_BLOCK
    once = dict(pipeline_mode=pl.Buffered(1))
    gate_spec = pl.BlockSpec((PEER_HEADS, PEER_NKEYS, tm), lambda i, c: (0, 0, i), **once)
    tok_spec = pl.BlockSpec((tm, D_MODEL), lambda i, c: (i, 0))
    return pl.pallas_call(
        _experts_kernel,
        grid=(t // tm, n_blk + 1),
        in_specs=[tok_spec,
                  pl.BlockSpec((EXPERT_BLOCK, D_MODEL), lambda i, c: (jnp.minimum(c, n_blk - 1), 0)),
                  pl.BlockSpec((D_MODEL, EXPERT_BLOCK), lambda i, c: (0, jnp.maximum(c - 1, 0))),
                  gate_spec, gate_spec, gate_spec, gate_spec,
                  pl.BlockSpec((tm, D_MODEL), lambda i, c: (i, 0), **once)],
        out_specs=tok_spec,
        out_shape=jax.ShapeDtypeStruct((t, D_MODEL), F32),
        scratch_shapes=[pltpu.VMEM((D_MODEL, tm), F32), pltpu.VMEM((EXPERT_BLOCK, tm), BF16)],
        compiler_params=_cparams(2), name="experts",
    )(xn, u_bf, vt_bf, nc1, r2, e1, e2, h1)


def _ple_kernel(h_ref, p_ref, gple_ref, wg_ref, wp_ref, gfin_ref, y_ref):
    h = h_ref[...]
    gate = _sigmoid(jnp.dot(_rmsnorm(h, gple_ref[...]).astype(BF16), wg_ref[...], preferred_element_type=F32))
    proj = jnp.dot(p_ref[...].astype(BF16), wp_ref[...], preferred_element_type=F32)
    y_ref[...] = _rmsnorm(h + gate * proj, gfin_ref[...])


def _ple_final(h2, p_emb, g_ple, wg_bf, wp_bf, g_final, *, tm):
    t = h2.shape[0]
    const = lambda shape: pl.BlockSpec(shape, lambda i: (0,) * len(shape))
    return pl.pallas_call(
        _ple_kernel,
        grid=(t // tm,),
        in_specs=[pl.BlockSpec((tm, D_MODEL), lambda i: (i, 0)), pl.BlockSpec((tm, PLE_DIM), lambda i: (i, 0)),
                  const((1, D_MODEL)), const((D_MODEL, D_MODEL)), const((PLE_DIM, D_MODEL)), const((1, D_MODEL))],
        out_specs=pl.BlockSpec((tm, D_MODEL), lambda i: (i, 0)),
        out_shape=jax.ShapeDtypeStruct((t, D_MODEL), F32),
        compiler_params=_cparams(1), name="ple",
    )(h2, p_emb, g_ple, wg_bf, wp_bf, g_final)


def _pick_tile(t, want):
    tm = min(want, t)
    assert t % tm == 0
    return tm


def _ffn(h1, p_emb, lw, g_final):
    t = h1.shape[0]
    xn, nc1, r2, e1, e2 = _peer_route(h1, lw["g_ffn"], lw["wq"], lw["keys"], tm=_pick_tile(t, 256))
    h2 = _peer_experts(xn, lw["u"], lw["vt"], nc1, r2, e1, e2, h1, tm=_pick_tile(t, 1024))
    return _ple_final(h2, p_emb, lw["g_ple"], lw["w_ple_gate"], lw["w_ple_proj"], g_final, tm=_pick_tile(t, 512))


def kernel(x_prompt, x_sample, cache_k, cache_v, cache_conv, page_table, p_prompt, p_sample,
           g_mix, w_in, w_dw, b_dw, ln_g, ln_b, w_out, g_ffn, peer_wq, peer_keys, peer_u, peer_v,
           g_ple, w_ple_gate, w_ple_proj, g_final):
    depth = g_mix.shape[0]
    batch, seq, _ = x_prompt.shape
    dec_batch, dec_seq, _ = x_sample.shape
    assert depth == 1 and batch == 1 and dec_seq == 1
    past_len = page_table.shape[1] * PAGE_SIZE
    assert seq % MOBA_BLOCK == 0 and past_len % MOBA_BLOCK == 0 and seq >= CONV_K - 1
    row = lambda a: a.reshape(1, -1)
    lane = jnp.arange(LANES) % (ROT_DIM // 2)
    inv_lane = (ROPE_THETA ** (-lane.astype(F32) * 2.0 / ROT_DIM)).reshape(1, LANES)
    l = 0
    lw = dict(
        g_ffn=row(g_ffn[l]), wq=peer_wq[l].astype(BF16),
        keys=peer_keys[l].reshape(2 * PEER_HEADS, PEER_NKEYS, -1).astype(BF16),
        u=peer_u[l].astype(BF16), vt=peer_v[l].T.astype(BF16),
        g_ple=row(g_ple[l]), w_ple_gate=w_ple_gate[l].astype(BF16), w_ple_proj=w_ple_proj[l].astype(BF16))
    w_in_bf = w_in[l].astype(BF16)
    w_out_bf = w_out[l].astype(BF16)
    w_dw_pad = jnp.pad(w_dw[l], ((0, CONV_HALO - CONV_K), (0, 0)))
    conv_args = (w_dw_pad, row(b_dw[l]), row(ln_g[l]), row(ln_b[l]), w_out_bf)
    g_fin = row(g_final)

    xp = x_prompt.reshape(seq, D_MODEL)
    k_p, v_p, glu_p, qp, kp, vt, kmean = _inproj(xp, row(g_mix[l]), w_in_bf, inv_lane, pos0=0, pos_stride=1,
                                                 tm=MOBA_BLOCK, with_aux=True)
    attn_p = _moba_prompt(qp, kp, vt, kmean)
    h1_p = _mix_prompt(glu_p, attn_p, xp, *conv_args, tm=_pick_tile(seq, 256))
    y_p = _ffn(h1_p, p_prompt[l].reshape(seq, PLE_DIM), lw, g_fin)

    xs = x_sample.reshape(dec_batch, D_MODEL)
    k_s, v_s, glu_s, q_s = _inproj(xs, row(g_mix[l]), w_in_bf, inv_lane, pos0=past_len, pos_stride=0,
                                   tm=dec_batch, with_aux=False)
    attn_s = _moba_decode(q_s, k_s, v_s, cache_k[l], cache_v[l], page_table)
    xin_s = jnp.concatenate([cache_conv[l], glu_s[:, None, :]], axis=1)
    xin_pad = jnp.pad(xin_s, ((0, 0), (0, CONV_HALO - CONV_K), (0, 0)))
    h1_s = _mix_decode(xin_pad, attn_s, xs, *conv_args)
    pad = (-dec_batch) % LANES
    h1_s = jnp.pad(h1_s, ((0, pad), (0, 0)))
    p_s = jnp.pad(p_sample[l].reshape(dec_batch, PLE_DIM), ((0, pad), (0, 0)))
    y_s = _ffn(h1_s, p_s, lw, g_fin)[:dec_batch]

    conv_p = glu_p[seq - (CONV_K - 1):].reshape(1, 1, CONV_K - 1, CONV_CH)
    conv_s = xin_s[:, 1:].reshape(1, dec_batch, CONV_K - 1, CONV_CH)
    return (y_p.reshape(1, seq, D_MODEL), y_s.reshape(dec_batch, 1, D_MODEL),
            k_p.reshape(1, 1, seq, N_HEADS, HEAD_DIM), v_p.reshape(1, 1, seq, N_HEADS, HEAD_DIM), conv_p,
            k_s.reshape(1, dec_batch, 1, N_HEADS, HEAD_DIM), v_s.reshape(1, dec_batch, 1, N_HEADS, HEAD_DIM),
            conv_s)
```
